```python
import math
import jax
import jax.numpy as jnp
from jax import lax
import numpy as np

D_MODEL = 1024
BATCH = 16
SEQ = 2048
DEPTH = 2

NORM_EPS = 1e-5
N_MOD = 6

HEAD_DIM = 64
N_Q_HEADS = 16
N_KV_HEADS = 4
Q_PER_KV = N_Q_HEADS // N_KV_HEADS
ATTN_WIDTH = N_Q_HEADS * HEAD_DIM
KV_WIDTH = N_KV_HEADS * HEAD_DIM
WINDOW = 128
ATTN_BLOCK = 128
ROPE_THETA = 500000.0
ROPE_DIMS = HEAD_DIM // 4
NEG_BIG = -1e30

SSM_INNER = 2 * D_MODEL
SSM_HEAD_DIM = 64
SSM_HEADS = SSM_INNER // SSM_HEAD_DIM
SSM_GROUPS = 4
SSM_HEADS_PER_GROUP = SSM_HEADS // SSM_GROUPS
SSM_STATE = 128
SSM_CONV = 5
SSM_CHUNK = 128
SSM_XBC = SSM_INNER + 2 * SSM_GROUPS * SSM_STATE

N_BRANCH = 2
N_EXPERTS = 32
TOP_K = 4
EXPERT_FF = D_MODEL
SWIGLU_LIMIT = 7.0
SWIGLU_ALPHA = 1.702

IN_SIZES = (ATTN_WIDTH, KV_WIDTH, KV_WIDTH, SSM_INNER, SSM_XBC, 2 * SSM_HEADS, N_BRANCH * D_MODEL)
IN_TOTAL = sum(IN_SIZES)
IN_OFFSETS = tuple(int(v) for v in np.cumsum(IN_SIZES)[:-1])

kernel_name = "hybrid_swa_ssd_moe_encoder"


def rms_norm(x, g):
    xf = x.astype(jnp.float32)
    y = xf * lax.rsqrt(jnp.mean(xf * xf, axis=-1, keepdims=True) + NORM_EPS)
    return (y * g.astype(jnp.float32)).astype(x.dtype)


def rope_tables(positions):
    inv_freq = ROPE_THETA ** (-jnp.arange(0, ROPE_DIMS, 2, dtype=jnp.float32) / ROPE_DIMS)
    ang = positions.astype(jnp.float32)[..., None] * inv_freq
    return jnp.cos(ang)[:, :, None, :], jnp.sin(ang)[:, :, None, :]


def apply_partial_rope(t, cos, sin):
    half = ROPE_DIMS // 2
    tf = t.astype(jnp.float32)
    t1, t2, rest = tf[..., :half], tf[..., half:ROPE_DIMS], tf[..., ROPE_DIMS:]
    return jnp.concatenate([t1 * cos - t2 * sin, t2 * cos + t1 * sin, rest], axis=-1).astype(t.dtype)


def windowed_gqa_with_sink(q, k, v, sink):
    bsz, s = q.shape[0], q.shape[1]
    nb = s // ATTN_BLOCK
    qb = q.reshape(bsz, nb, ATTN_BLOCK, N_KV_HEADS, Q_PER_KV, HEAD_DIM)
    pad = ((0, 0), (ATTN_BLOCK, ATTN_BLOCK), (0, 0), (0, 0))
    kp = jnp.pad(k, pad).reshape(bsz, nb + 2, ATTN_BLOCK, N_KV_HEADS, HEAD_DIM)
    vp = jnp.pad(v, pad).reshape(bsz, nb + 2, ATTN_BLOCK, N_KV_HEADS, HEAD_DIM)
    kb = jnp.concatenate([kp[:, :-2], kp[:, 1:-1], kp[:, 2:]], axis=2)
    vb = jnp.concatenate([vp[:, :-2], vp[:, 1:-1], vp[:, 2:]], axis=2)
    scores = jnp.einsum('bnqkrd,bnskd->bnkrqs', qb, kb).astype(jnp.float32) * (HEAD_DIM ** -0.5)
    blk = jnp.arange(nb)[:, None, None] * ATTN_BLOCK
    q_idx = blk + jnp.arange(ATTN_BLOCK)[None, :, None]
    k_idx = blk - ATTN_BLOCK + jnp.arange(3 * ATTN_BLOCK)[None, None, :]
    valid = (jnp.abs(q_idx - k_idx) <= WINDOW) & (k_idx >= 0) & (k_idx < s)
    scores = jnp.where(valid[None, :, None, None], scores, NEG_BIG)
    sink_b = sink.astype(jnp.float32).reshape(N_KV_HEADS, Q_PER_KV)[None, None, :, :, None, None]
    m = jnp.maximum(jnp.max(scores, axis=-1, keepdims=True), sink_b)
    p = jnp.exp(scores - m)
    p = p / (jnp.sum(p, axis=-1, keepdims=True) + jnp.exp(sink_b - m))
    out = jnp.einsum('bnkrqs,bnskd->bnqkrd', p.astype(v.dtype), vb)
    return out.reshape(bsz, s, ATTN_WIDTH)


def centred_depthwise_conv(u, w, b):
    out = lax.conv_general_dilated(
        u, w[:, None, :].astype(u.dtype), window_strides=(1,),
        padding=[(SSM_CONV // 2, SSM_CONV // 2)],
        dimension_numbers=('NWC', 'WIO', 'NWC'), feature_group_count=u.shape[-1])
    return out + b.astype(u.dtype)


def ssd_scan(xs, dt, a, bs, cs):
    bsz, s = xs.shape[0], xs.shape[1]
    nc = s // SSM_CHUNK
    q, g, r = SSM_CHUNK, SSM_GROUPS, SSM_HEADS_PER_GROUP
    xdt = (xs * dt[..., None]).reshape(bsz, nc, q, g, r, SSM_HEAD_DIM)
    da = (dt * a).reshape(bsz, nc, q, g, r).transpose(0, 3, 4, 1, 2)
    bc = bs.reshape(bsz, nc, q, g, SSM_STATE)
    cc = cs.reshape(bsz, nc, q, g, SSM_STATE)
    a_cum = jnp.cumsum(da, axis=-1)
    t = jnp.arange(q)
    lower = t[:, None] >= t[None, :]
    decay_in = jnp.exp(jnp.where(lower, a_cum[..., :, None] - a_cum[..., None, :], -jnp.inf))
    cb = jnp.einsum('bclgn,bcsgn->bgcls', cc, bc)
    y_diag = jnp.einsum('bgrcls,bcsgrp->bclgrp', cb[:, :, None] * decay_in, xdt)
    decay_to_end = jnp.exp(a_cum[..., -1:] - a_cum)
    states = jnp.einsum('bclgn,bgrcl,bclgrp->bcgrpn', bc, decay_to_end, xdt)
    chunk_sum = jnp.pad(a_cum[..., -1], ((0, 0), (0, 0), (0, 0), (1, 0)))
    ccum = jnp.cumsum(chunk_sum, axis=-1)
    zi = jnp.arange(nc + 1)
    decay_chunk = jnp.exp(jnp.where(zi[:, None] >= zi[None, :], ccum[..., :, None] - ccum[..., None, :], -jnp.inf))
    states = jnp.concatenate([jnp.zeros_like(states[:, :1]), states], axis=1)
    states_in = jnp.einsum('bgrzc,bcgrpn->bzgrpn', decay_chunk, states)[:, :-1]
    y_off = jnp.einsum('bclgn,bcgrpn,bgrcl->bclgrp', cc, states_in, jnp.exp(a_cum))
    return (y_diag + y_off).reshape(bsz, s, SSM_HEADS, SSM_HEAD_DIM)


def _flip(u):
    return jnp.flip(u, axis=1)


def bidirectional_ssd_mixer(z, xbc, dt_raw, conv_w, conv_b, a_log, dt_bias, d_skip, norm_g):
    bsz, s = z.shape[0], z.shape[1]
    xbc = jax.nn.silu(centred_depthwise_conv(xbc, conv_w, conv_b)).astype(jnp.float32)
    xs, bs, cs = jnp.split(xbc, [SSM_INNER, SSM_INNER + SSM_GROUPS * SSM_STATE], axis=-1)
    xs = xs.reshape(bsz, s, SSM_HEADS, SSM_HEAD_DIM)
    bs = bs.reshape(bsz, s, SSM_GROUPS, SSM_STATE)
    cs = cs.reshape(bsz, s, SSM_GROUPS, SSM_STATE)
    dt = jax.nn.softplus(dt_raw.astype(jnp.float32).reshape(bsz, s, 2, SSM_HEADS) + dt_bias.astype(jnp.float32))
    a = -jnp.exp(a_log.astype(jnp.float32))
    y = (ssd_scan(xs, dt[:, :, 0], a[0], bs, cs)
         + _flip(ssd_scan(_flip(xs), _flip(dt[:, :, 1]), a[1], _flip(bs), _flip(cs)))
         + xs * d_skip.astype(jnp.float32)[:, None])
    y = y.reshape(bsz, s, SSM_INNER) * jax.nn.silu(z.astype(jnp.float32))
    y = y.reshape(bsz, s, SSM_GROUPS, SSM_INNER // SSM_GROUPS)
    y = y * lax.rsqrt(jnp.mean(y * y, axis=-1, keepdims=True) + NORM_EPS)
    return (y.reshape(bsz, s, SSM_INNER) * norm_g.astype(jnp.float32)).astype(z.dtype)


def hybrid_mixer(h, cos, sin, w_in, q_norm_g, k_norm_g, attn_sink, conv_w, conv_b,
                 a_log, dt_bias, ssm_d, ssm_norm_g, w_attn_o, w_ssm_o, w_out):
    bsz, s = h.shape[0], h.shape[1]
    proj = jnp.einsum('bsd,de->bse', h, w_in)
    q, k, v, z, xbc, dt_raw, gates = jnp.split(proj, IN_OFFSETS, axis=-1)
    q = apply_partial_rope(rms_norm(q.reshape(bsz, s, N_Q_HEADS, HEAD_DIM), q_norm_g), cos, sin)
    k = apply_partial_rope(rms_norm(k.reshape(bsz, s, N_KV_HEADS, HEAD_DIM), k_norm_g), cos, sin)
    v = v.reshape(bsz, s, N_KV_HEADS, HEAD_DIM)
    y_attn = jnp.einsum('bse,ed->bsd', windowed_gqa_with_sink(q, k, v, attn_sink), w_attn_o)
    y_ssm = jnp.einsum('bse,ed->bsd',
                       bidirectional_ssd_mixer(z, xbc, dt_raw, conv_w, conv_b, a_log, dt_bias, ssm_d, ssm_norm_g),
                       w_ssm_o)
    g_attn, g_ssm = jnp.split(jax.nn.sigmoid(gates), N_BRANCH, axis=-1)
    return jnp.einsum('bsd,de->bse', g_attn * y_attn + g_ssm * y_ssm, w_out)


def moe_clamped_swiglu(h, router_w, router_b, w_gate, b_gate, w_up, b_up, w_down, b_down):
    bsz, s, d = h.shape
    tok = h.reshape(bsz * s, d)
    logits = (tok @ router_w + router_b).astype(jnp.float32)
    top_v, top_i = lax.top_k(logits, TOP_K)
    top_w = jax.nn.softmax(top_v, axis=-1)
    combine = jnp.sum(jax.nn.one_hot(top_i, N_EXPERTS, dtype=jnp.float32) * top_w[..., None], axis=1).astype(h.dtype)
    out = jnp.zeros_like(tok)
    for e in range(N_EXPERTS):
        glu = jnp.minimum(tok @ w_gate[e] + b_gate[e], SWIGLU_LIMIT)
        lin = jnp.clip(tok @ w_up[e] + b_up[e], -SWIGLU_LIMIT, SWIGLU_LIMIT)
        act = glu * jax.nn.sigmoid(SWIGLU_ALPHA * glu) * (lin + 1.0)
        out = out + combine[:, e:e + 1] * (act @ w_down[e] + b_down[e])
    return out.reshape(bsz, s, d)


def setup_inputs(seed: int = 0) -> dict:
    key = jax.random.key(seed)
    ks = jax.random.split(key, 28)
    f32 = jnp.float32
    L = DEPTH

    def nrm(k, shape, scale):
        return jax.random.normal(k, shape, f32) * scale

    offsets = jax.random.randint(ks[2], (BATCH, 1), 0, 4096, dtype=jnp.int32)
    positions = jnp.arange(SEQ, dtype=jnp.int32)[None, :] + offsets
    dt0 = jnp.exp(jax.random.uniform(ks[13], (L, 2, SSM_HEADS), f32, math.log(1e-3), math.log(1e-1)))
    return {
        'x': nrm(ks[0], (BATCH, SEQ, D_MODEL), 1.0),
        'c': nrm(ks[1], (BATCH, D_MODEL), 1.0),
        'positions': positions,
        'ada_w': nrm(ks[3], (L, D_MODEL, N_MOD * D_MODEL), 0.5 * D_MODEL ** -0.5),
        'ada_b': nrm(ks[4], (L, N_MOD * D_MODEL), 0.02),
        'norm1_g': 1.0 + nrm(ks[5], (L, D_MODEL), 0.02),
        'norm2_g': 1.0 + nrm(ks[6], (L, D_MODEL), 0.02),
        'w_in': nrm(ks[7], (L, D_MODEL, IN_TOTAL), D_MODEL ** -0.5),
        'q_norm_g': 1.0 + nrm(ks[8], (L, HEAD_DIM), 0.02),
        'k_norm_g': 1.0 + nrm(ks[9], (L, HEAD_DIM), 0.02),
        'attn_sink': nrm(ks[10], (L, N_Q_HEADS), 0.5),
        'conv_w': nrm(ks[11], (L, SSM_CONV, SSM_XBC), SSM_CONV ** -0.5),
        'conv_b': nrm(ks[12], (L, SSM_XBC), 0.02),
        'a_log': jnp.log(jax.random.uniform(ks[14], (L, 2, SSM_HEADS), f32, 1.0, 16.0)),
        'dt_bias': dt0 + jnp.log(-jnp.expm1(-dt0)),
        'ssm_d': 1.0 + nrm(ks[15], (L, SSM_HEADS), 0.1),
        'ssm_norm_g': 1.0 + nrm(ks[16], (L, SSM_INNER), 0.02),
        'w_attn_o': nrm(ks[17], (L, ATTN_WIDTH, D_MODEL), ATTN_WIDTH ** -0.5),
        'w_ssm_o': nrm(ks[18], (L, SSM_INNER, D_MODEL), SSM_INNER ** -0.5),
        'w_out': nrm(ks[19], (L, D_MODEL, D_MODEL), D_MODEL ** -0.5),
        'router_w': nrm(ks[20], (L, D_MODEL, N_EXPERTS), D_MODEL ** -0.5),
        'router_b': nrm(ks[21], (L, N_EXPERTS), 0.01),
        'exp_w_gate': nrm(ks[22], (L, N_EXPERTS, D_MODEL, EXPERT_FF), D_MODEL ** -0.5),
        'exp_b_gate': nrm(ks[23], (L, N_EXPERTS, EXPERT_FF), 0.01),
        'exp_w_up': nrm(ks[24], (L, N_EXPERTS, D_MODEL, EXPERT_FF), D_MODEL ** -0.5),
        'exp_b_up': nrm(ks[25], (L, N_EXPERTS, EXPERT_FF), 0.01),
        'exp_w_down': nrm(ks[26], (L, N_EXPERTS, EXPERT_FF, D_MODEL), EXPERT_FF ** -0.5),
        'exp_b_down': nrm(ks[27], (L, N_EXPERTS, D_MODEL), 0.01),
    }


def reference(x, c, positions, ada_w, ada_b, norm1_g, norm2_g, w_in, q_norm_g, k_norm_g,
              attn_sink, conv_w, conv_b, a_log, dt_bias, ssm_d, ssm_norm_g, w_attn_o,
              w_ssm_o, w_out, router_w, router_b, exp_w_gate, exp_b_gate, exp_w_up,
              exp_b_up, exp_w_down, exp_b_down):
    cos, sin = rope_tables(positions)
    c_act = jax.nn.silu(c)
    for l in range(DEPTH):
        mod = jnp.einsum('bd,de->be', c_act, ada_w[l]) + ada_b[l]
        sh1, sc1, g1, sh2, sc2, g2 = jnp.split(mod[:, None, :], N_MOD, axis=-1)
        h = rms_norm(x, norm1_g[l]) * (1.0 + sc1) + sh1
        x = x + g1 * hybrid_mixer(h, cos, sin, w_in[l], q_norm_g[l], k_norm_g[l], attn_sink[l],
                                  conv_w[l], conv_b[l], a_log[l], dt_bias[l], ssm_d[l],
                                  ssm_norm_g[l], w_attn_o[l], w_ssm_o[l], w_out[l])
        h = rms_norm(x, norm2_g[l]) * (1.0 + sc2) + sh2
        x = x + g2 * moe_clamped_swiglu(h, router_w[l], router_b[l], exp_w_gate[l], exp_b_gate[l],
                                        exp_w_up[l], exp_b_up[l], exp_w_down[l], exp_b_down[l])
    return x
```

```python
import functools

import jax
import jax.numpy as jnp
import numpy as np
from jax import lax
from jax.experimental import pallas as pl
from jax.experimental.pallas import tpu as pltpu

F32 = jnp.float32
BF16 = jnp.bfloat16
HIGHEST = lax.Precision.HIGHEST

LANES = 128
SUBLANES = 8
VMEM_LIMIT = 56 * 1024 * 1024

D_MODEL = 1024
NORM_EPS = 1e-5
N_MOD = 6

HEAD_DIM = 64
N_Q_HEADS = 16
N_KV_HEADS = 4
ATTN_WIDTH = N_Q_HEADS * HEAD_DIM
KV_WIDTH = N_KV_HEADS * HEAD_DIM
WINDOW = 128
ATTN_BLOCK = 128
ROPE_THETA = 500000.0
ROPE_DIMS = HEAD_DIM // 4
ROPE_HALF = ROPE_DIMS // 2
NEG_BIG = -1e30

SSM_INNER = 2 * D_MODEL
SSM_HEAD_DIM = 64
SSM_HEADS = SSM_INNER // SSM_HEAD_DIM
SSM_GROUPS = 4
SSM_HPG = SSM_HEADS // SSM_GROUPS
SSM_STATE = 128
SSM_CONV = 5
SSM_CHUNK = 128
GROUP_WIDTH = SSM_INNER // SSM_GROUPS
DT_COLS = 2 * SSM_HPG

N_EXPERTS = 32
TOP_K = 4
SWIGLU_LIMIT = 7.0
SWIGLU_ALPHA = 1.702

COL_Q = 0
COL_GATE_ATTN = 1024
COL_GATE_SSM = 2048
COL_Z = 3072
COL_XS = 5120
COL_K = 7168
COL_V = 7424
COL_B = 7680
COL_C = 8192
PROJ_WIDTH = 8704
PROJ_N_TILE = PROJ_WIDTH // 4

_O_Q, _O_K, _O_V, _O_Z, _O_XBC, _O_DT, _O_GATES = 0, 1024, 1280, 1536, 3584, 6656, 6720


def _params(*sem):
    return pltpu.CompilerParams(dimension_semantics=sem, vmem_limit_bytes=VMEM_LIMIT)


def _prenorm_modulate(x, gain, shift, scale):
    ms = jnp.mean(x * x, axis=-1, keepdims=True)
    return (x * lax.rsqrt(ms + NORM_EPS) * gain) * (1.0 + scale) + shift


def _ada_kernel(c_ref, w_ref, b_ref, o_ref):
    c = c_ref[...]
    c_act = c * (1.0 / (1.0 + jnp.exp(-c)))
    o_ref[0] = jnp.dot(c_act, w_ref[0], precision=HIGHEST, preferred_element_type=F32) + b_ref[0]


def _ada_call(c, ada_w, ada_b):
    depth, d, n = ada_w.shape
    bsz = c.shape[0]
    tn = 1536
    return pl.pallas_call(
        _ada_kernel,
        grid=(depth, n // tn),
        in_specs=[
            pl.BlockSpec((bsz, d), lambda l, j: (0, 0)),
            pl.BlockSpec((1, d, tn), lambda l, j: (l, 0, j)),
            pl.BlockSpec((1, 1, tn), lambda l, j: (l, 0, j)),
        ],
        out_specs=pl.BlockSpec((1, bsz, tn), lambda l, j: (l, 0, j)),
        out_shape=jax.ShapeDtypeStruct((depth, bsz, n), F32),
        compiler_params=_params("arbitrary", "arbitrary"),
        name="ada_mod",
    )(c, ada_w, ada_b.reshape(depth, 1, n))


def _rope_kernel(pos_ref, freq_ref, sign_ref, cos_ref, sin_ref):
    ang = pos_ref[0].astype(F32) * freq_ref[...]
    sign = sign_ref[...]
    cos_ref[0] = jnp.where(sign == 0.0, 1.0, jnp.cos(ang))
    sin_ref[0] = jnp.sin(ang) * sign


def _rope_call(positions):
    bsz, s = positions.shape
    lane = np.arange(LANES) % HEAD_DIM
    inv_freq = ROPE_THETA ** (-np.arange(0, ROPE_DIMS, 2, dtype=np.float32) / ROPE_DIMS)
    freq = np.where(lane < ROPE_DIMS, inv_freq[lane % ROPE_HALF], 0.0).astype(np.float32)
    sign = np.where(lane < ROPE_HALF, -1.0, np.where(lane < ROPE_DIMS, 1.0, 0.0)).astype(np.float32)
    ts = 512
    spec = pl.BlockSpec((1, ts, LANES), lambda b, i: (b, i, 0))
    return pl.pallas_call(
        _rope_kernel,
        grid=(bsz, s // ts),
        in_specs=[
            pl.BlockSpec((1, ts, 1), lambda b, i: (b, i, 0)),
            pl.BlockSpec((1, LANES), lambda b, i: (0, 0)),
            pl.BlockSpec((1, LANES), lambda b, i: (0, 0)),
        ],
        out_specs=[spec, spec],
        out_shape=[jax.ShapeDtypeStruct((bsz, s, LANES), F32)] * 2,
        compiler_params=_params("arbitrary", "arbitrary"),
        name="rope_tables",
    )(positions.reshape(bsz, s, 1), jnp.asarray(freq).reshape(1, LANES), jnp.asarray(sign).reshape(1, LANES))


def _inproj_kernel(x_ref, mod_ref, g_ref, w_ref, o_ref):
    h = _prenorm_modulate(x_ref[...], g_ref[...], mod_ref[0, 0:1, :], mod_ref[0, 1:2, :])
    o_ref[...] = jnp.dot(h.astype(BF16), w_ref[...], preferred_element_type=F32).astype(BF16)


def _inproj_call(x2, mod, gain, w_main, seq):
    t, d = x2.shape
    tm = 512
    per_batch = seq // tm
    return pl.pallas_call(
        _inproj_kernel,
        grid=(PROJ_WIDTH // PROJ_N_TILE, t // tm),
        in_specs=[
            pl.BlockSpec((tm, d), lambda n, i: (i, 0)),
            pl.BlockSpec((1, N_MOD, d), lambda n, i: (i // per_batch, 0, 0)),
            pl.BlockSpec((1, d), lambda n, i: (0, 0)),
            pl.BlockSpec((d, PROJ_N_TILE), lambda n, i: (0, n)),
        ],
        out_specs=pl.BlockSpec((tm, PROJ_N_TILE), lambda n, i: (i, n)),
        out_shape=jax.ShapeDtypeStruct((t, PROJ_WIDTH), BF16),
        compiler_params=_params("arbitrary", "arbitrary"),
        name="in_proj",
    )(x2, mod, gain, w_main)


def _dt_kernel(x_ref, mod_ref, g_ref, w_ref, bias_ref, alog_ref, cum_ref, rowr_ref, wv_ref, ev_ref, dch_ref):
    q = SSM_CHUNK
    h = _prenorm_modulate(x_ref[...], g_ref[...], mod_ref[0, 0:1, :], mod_ref[0, 1:2, :])
    raw = jnp.dot(h, w_ref[...], precision=HIGHEST, preferred_element_type=F32) + bias_ref[...]
    dt = jnp.maximum(raw, 0.0) + jnp.log1p(jnp.exp(-jnp.abs(raw)))
    da = dt * (-jnp.exp(alog_ref[...]))
    row = lax.broadcasted_iota(jnp.int32, (q, q), 0)
    col = lax.broadcasted_iota(jnp.int32, (q, q), 1)
    prefix = jnp.dot((row >= col).astype(F32), da, precision=HIGHEST, preferred_element_type=F32)
    suffix = jnp.dot((row <= col).astype(F32), da, precision=HIGHEST, preferred_element_type=F32)
    lane = lax.broadcasted_iota(jnp.int32, (1, LANES), 1)
    is_fwd = (lane % DT_COLS) < SSM_HPG
    cum = jnp.where(is_fwd, prefix, suffix)
    tot = jnp.where(is_fwd, prefix[q - 1:q, :], suffix[0:1, :])
    ev = jnp.exp(cum)
    wv = jnp.exp(tot - cum) * dt
    rowr_t = (cum - jnp.log(dt)).T
    for g in range(SSM_GROUPS):
        cs = slice(g * DT_COLS, (g + 1) * DT_COLS)
        cum_ref[0, g] = cum[:, cs]
        wv_ref[0, g] = wv[:, cs]
        ev_ref[0, g] = ev[:, cs]
        rowr_ref[0, g] = rowr_t[cs, :]
    dch_ref[0, 0] = jnp.exp(tot)


def _dt_call(x2, mod, gain, w_dt, dt_bias, a_log, bsz, seq):
    t, d = x2.shape
    q = SSM_CHUNK
    nc = seq // q
    col_spec = pl.BlockSpec((1, SSM_GROUPS, q, DT_COLS), lambda b, c: (b, 0, c, 0))
    col_shape = jax.ShapeDtypeStruct((bsz, SSM_GROUPS, seq, DT_COLS), F32)
    return pl.pallas_call(
        _dt_kernel,
        grid=(bsz, nc),
        in_specs=[
            pl.BlockSpec((q, d), lambda b, c: (b * nc + c, 0)),
            pl.BlockSpec((1, N_MOD, d), lambda b, c: (b, 0, 0)),
            pl.BlockSpec((1, d), lambda b, c: (0, 0)),
            pl.BlockSpec((d, LANES), lambda b, c: (0, 0)),
            pl.BlockSpec((1, LANES), lambda b, c: (0, 0)),
            pl.BlockSpec((1, LANES), lambda b, c: (0, 0)),
        ],
        out_specs=[
            col_spec,
            pl.BlockSpec((1, SSM_GROUPS, DT_COLS, q), lambda b, c: (b, 0, 0, c)),
            col_spec,
            col_spec,
            pl.BlockSpec((1, 1, 1, LANES), lambda b, c: (b, c, 0, 0)),
        ],
        out_shape=[
            col_shape,
            jax.ShapeDtypeStruct((bsz, SSM_GROUPS, DT_COLS, seq), F32),
            col_shape,
            col_shape,
            jax.ShapeDtypeStruct((bsz, nc, 1, LANES), F32),
        ],
        compiler_params=_params("arbitrary", "arbitrary"),
        name="dt_prep",
    )(x2, mod, gain, w_dt, dt_bias, a_log)


def _head_rms_rope(t, gain, cos, sin, lane):
    lo = lane < HEAD_DIM
    sq = t * t
    ss_lo = jnp.sum(jnp.where(lo, sq, 0.0), axis=-1, keepdims=True)
    ss_hi = jnp.sum(jnp.where(lo, 0.0, sq), axis=-1, keepdims=True)
    r = jnp.where(lo, lax.rsqrt(ss_lo * (1.0 / HEAD_DIM) + NORM_EPS), lax.rsqrt(ss_hi * (1.0 / HEAD_DIM) + NORM_EPS))
    tn = t * r * gain
    first_half = (lane % HEAD_DIM) < ROPE_HALF
    partner = jnp.where(first_half, pltpu.roll(tn, LANES - ROPE_HALF, 1), pltpu.roll(tn, ROPE_HALF, 1))
    return tn * cos + partner * sin


def _attn_kernel(sink_ref, q_ref, k_ref, v_ref, cos_ref, sin_ref, qg_ref, kg_ref, o_ref, kpad, vpad):
    blk = ATTN_BLOCK
    s = q_ref.shape[0]
    nb = s // blk
    lane = lax.broadcasted_iota(jnp.int32, (1, LANES), 1)
    lo = lane < HEAD_DIM

    zero_blk = jnp.zeros((blk, LANES), BF16)
    for j in range(2 * N_KV_HEADS):
        kpad[j, 0:blk, :] = zero_blk
        kpad[j, blk + s:2 * blk + s, :] = zero_blk
        vpad[j, 0:blk, :] = zero_blk
        vpad[j, blk + s:2 * blk + s, :] = zero_blk

    def prep(c, carry):
        r0 = pl.multiple_of(c * blk, blk)
        rows = pl.ds(r0, blk)
        dst = pl.ds(r0 + blk, blk)
        cos = cos_ref[0, rows, :]
        sin = sin_ref[0, rows, :]
        for tpair in range(N_KV_HEADS // 2):
            ls = slice(tpair * LANES, (tpair + 1) * LANES)
            kr = _head_rms_rope(k_ref[rows, ls].astype(F32), kg_ref[...], cos, sin, lane)
            vv = v_ref[rows, ls].astype(F32)
            for src, store in ((kr, kpad), (vv, vpad)):
                even_lo = jnp.where(lo, src, 0.0)
                odd_hi = jnp.where(lo, 0.0, src)
                g0, g1 = 2 * tpair, 2 * tpair + 1
                store[2 * g0, dst, :] = even_lo.astype(BF16)
                store[2 * g0 + 1, dst, :] = pltpu.roll(even_lo, HEAD_DIM, 1).astype(BF16)
                store[2 * g1, dst, :] = pltpu.roll(odd_hi, HEAD_DIM, 1).astype(BF16)
                store[2 * g1 + 1, dst, :] = odd_hi.astype(BF16)
        return carry

    lax.fori_loop(0, nb, prep, 0)

    rowi = lax.broadcasted_iota(jnp.int32, (2 * blk, 3 * blk), 0) % blk
    coli = lax.broadcasted_iota(jnp.int32, (2 * blk, 3 * blk), 1)
    rel = coli - rowi
    in_band = (rel >= 0) & (rel <= 2 * WINDOW)
    top_rows = lax.broadcasted_iota(jnp.int32, (2 * blk, 1), 0) < blk
    scale = HEAD_DIM ** -0.5

    def qblock(n, carry):
        r0 = pl.multiple_of(n * blk, blk)
        rows = pl.ds(r0, blk)
        band = pl.ds(r0, 3 * blk)
        cos = cos_ref[0, rows, :]
        sin = sin_ref[0, rows, :]
        kidx = coli + (r0 - blk)
        bias = jnp.where(in_band & (kidx >= 0) & (kidx < s), 0.0, NEG_BIG)
        for g in range(N_KV_HEADS):
            qs = []
            for pair in (2 * g, 2 * g + 1):
                qt = q_ref[rows, pair * LANES:(pair + 1) * LANES].astype(F32)
                qs.append(_head_rms_rope(qt, qg_ref[...], cos, sin, lane) * scale)
            qq = jnp.concatenate(qs, axis=0).astype(BF16)
            acc = jnp.zeros((2 * blk, LANES), F32)
            for par in range(2):
                kb = kpad[2 * g + par, band, :]
                sc = lax.dot_general(qq, kb, (((1,), (1,)), ((), ())), preferred_element_type=F32) + bias
                sink = jnp.where(top_rows, sink_ref[4 * g + par], sink_ref[4 * g + 2 + par])
                m = jnp.maximum(jnp.max(sc, axis=-1, keepdims=True), sink)
                p = jnp.exp(sc - m)
                denom = jnp.sum(p, axis=-1, keepdims=True) + jnp.exp(sink - m)
                pv = jnp.dot(p.astype(BF16), vpad[2 * g + par, band, :], preferred_element_type=F32)
                acc = acc + pv * (1.0 / denom)
            o_ref[rows, (2 * g) * LANES:(2 * g + 1) * LANES] = acc[0:blk].astype(BF16)
            o_ref[rows, (2 * g + 1) * LANES:(2 * g + 2) * LANES] = acc[blk:2 * blk].astype(BF16)
        return carry

    lax.fori_loop(0, nb, qblock, 0)


def _attn_call(proj, cos_t, sin_t, q_gain, k_gain, sink, bsz, seq):
    t = proj.shape[0]
    return pl.pallas_call(
        _attn_kernel,
        grid_spec=pltpu.PrefetchScalarGridSpec(
            num_scalar_prefetch=1,
            grid=(bsz,),
            in_specs=[
                pl.BlockSpec((seq, ATTN_WIDTH), lambda b, sk: (b, COL_Q // ATTN_WIDTH)),
                pl.BlockSpec((seq, KV_WIDTH), lambda b, sk: (b, COL_K // KV_WIDTH)),
                pl.BlockSpec((seq, KV_WIDTH), lambda b, sk: (b, COL_V // KV_WIDTH)),
                pl.BlockSpec((1, seq, LANES), lambda b, sk: (b, 0, 0)),
                pl.BlockSpec((1, seq, LANES), lambda b, sk: (b, 0, 0)),
                pl.BlockSpec((1, LANES), lambda b, sk: (0, 0)),
                pl.BlockSpec((1, LANES), lambda b, sk: (0, 0)),
            ],
            out_specs=pl.BlockSpec((seq, ATTN_WIDTH), lambda b, sk: (b, 0)),
            scratch_shapes=[
                pltpu.VMEM((2 * N_KV_HEADS, seq + 2 * ATTN_BLOCK, LANES), BF16),
                pltpu.VMEM((2 * N_KV_HEADS, seq + 2 * ATTN_BLOCK, LANES), BF16),
            ],
        ),
        out_shape=jax.ShapeDtypeStruct((t, ATTN_WIDTH), BF16),
        compiler_params=_params("arbitrary"),
        name="window_attn",
    )(sink, proj, proj, proj, cos_t, sin_t, q_gain, k_gain)


def _ssd_kernel(dch_ref, xs_ref, b_ref, c_ref, z_ref, cwx_ref, cwb_ref, cwc_ref, cbx_ref, cbb_ref, cbc_ref,
                cum_ref, rowr_ref, wv_ref, ev_ref, dskip_ref, ng_ref, o_ref,
                upad, xs_c, b_c, c_c, y_acc, st):
    q = SSM_CHUNK
    s = xs_ref.shape[0]
    nc = s // q
    pad = 16
    halo = SSM_CONV // 2
    win = q + pad
    width = GROUP_WIDTH + 2 * SSM_STATE

    upad[0:pad, :] = jnp.zeros((pad, width), F32)
    upad[pad + s:2 * pad + s, :] = jnp.zeros((pad, width), F32)

    def fill(c, carry):
        r0 = pl.multiple_of(c * q, q)
        rows = pl.ds(r0, q)
        dst = pl.ds(r0 + pad, q)
        upad[dst, 0:GROUP_WIDTH] = xs_ref[rows, :].astype(F32)
        upad[dst, GROUP_WIDTH:GROUP_WIDTH + SSM_STATE] = b_ref[rows, :].astype(F32)
        upad[dst, GROUP_WIDTH + SSM_STATE:width] = c_ref[rows, :].astype(F32)
        return carry

    lax.fori_loop(0, nc, fill, 0)

    def conv_cols(r0, lo_col, n_col, w_ref, bias_ref, dst_ref):
        window = upad[pl.ds(r0 + pad - SUBLANES, win), lo_col:lo_col + n_col]
        acc = jnp.zeros((q, n_col), F32) + bias_ref[...]
        for k in range(SSM_CONV):
            shift = SUBLANES - halo + k
            acc = acc + pltpu.roll(window, win - shift, 0)[0:q] * w_ref[k:k + 1, :]
        dst_ref[pl.ds(r0, q), :] = acc * (1.0 / (1.0 + jnp.exp(-acc)))

    def conv(c, carry):
        r0 = pl.multiple_of(c * q, q)
        for j in range(GROUP_WIDTH // LANES):
            conv_cols(r0, j * LANES, LANES, cwx_ref.at[:, j * LANES:(j + 1) * LANES],
                      cbx_ref.at[:, j * LANES:(j + 1) * LANES], xs_c.at[:, j * LANES:(j + 1) * LANES])
        conv_cols(r0, GROUP_WIDTH, SSM_STATE, cwb_ref, cbb_ref, b_c)
        conv_cols(r0, GROUP_WIDTH + SSM_STATE, SSM_STATE, cwc_ref, cbc_ref, c_c)
        return carry

    lax.fori_loop(0, nc, conv, 0)

    row = lax.broadcasted_iota(jnp.int32, (q, q), 0)
    col = lax.broadcasted_iota(jnp.int32, (q, q), 1)
    lower = row >= col
    upper = row <= col

    def head_cols(h):
        return slice(h * SSM_HEAD_DIM, (h + 1) * SSM_HEAD_DIM)

    def scale_heads(x, cols, first):
        return jnp.concatenate(
            [x[:, head_cols(h)] * cols[:, first + h:first + h + 1] for h in range(SSM_HPG)], axis=-1)

    def decay_rows(c, first):
        return jnp.concatenate(
            [jnp.full((1, SSM_HEAD_DIM), dch_ref[0, 0, c, first + h], F32) for h in range(SSM_HPG)], axis=-1)

    st[...] = jnp.zeros_like(st)

    def fwd(c, carry):
        r0 = pl.multiple_of(c * q, q)
        rows = pl.ds(r0, q)
        xs = xs_c[rows, :]
        bm = b_c[rows, :]
        cm = c_c[rows, :].astype(BF16)
        cum = cum_ref[0, 0, rows, :]
        rowr = rowr_ref[0, 0, :, rows]
        wv = wv_ref[0, 0, rows, :]
        ev = ev_ref[0, 0, rows, :]
        cb = lax.dot_general(cm, bm.astype(BF16), (((1,), (1,)), ((), ())), preferred_element_type=F32)
        xs_b = xs.astype(BF16)
        ys = []
        for h in range(SSM_HPG):
            ef = jnp.exp(jnp.where(lower, cum[:, h:h + 1] - rowr[h:h + 1, :], NEG_BIG))
            eb = jnp.exp(jnp.where(upper, cum[:, SSM_HPG + h:SSM_HPG + h + 1] - rowr[SSM_HPG + h:SSM_HPG + h + 1, :],
                                   NEG_BIG))
            m = (cb * (ef + eb)).astype(BF16)
            ys.append(jnp.dot(m, xs_b[:, head_cols(h)], preferred_element_type=F32))
        y = jnp.concatenate(ys, axis=-1) + xs * dskip_ref[...]
        state = st[...]
        y_off = jnp.dot(cm, state.astype(BF16), preferred_element_type=F32)
        y_acc[rows, :] = y + scale_heads(y_off, ev, 0)
        xw = scale_heads(xs, wv, 0).astype(BF16)
        upd = jnp.dot(bm.T.astype(BF16), xw, preferred_element_type=F32)
        st[...] = state * decay_rows(c, 0) + upd
        return carry

    lax.fori_loop(0, nc, fwd, 0)

    st[...] = jnp.zeros_like(st)

    def bwd(i, carry):
        c = nc - 1 - i
        r0 = pl.multiple_of(c * q, q)
        rows = pl.ds(r0, q)
        xs = xs_c[rows, :]
        bm = b_c[rows, :]
        cm = c_c[rows, :].astype(BF16)
        wv = wv_ref[0, 0, rows, :]
        ev = ev_ref[0, 0, rows, :]
        state = st[...]
        y_off = jnp.dot(cm, state.astype(BF16), preferred_element_type=F32)
        y = y_acc[rows, :] + scale_heads(y_off, ev, SSM_HPG)
        xw = scale_heads(xs, wv, SSM_HPG).astype(BF16)
        upd = jnp.dot(bm.T.astype(BF16), xw, preferred_element_type=F32)
        st[...] = state * decay_rows(c, SSM_HPG) + upd
        zz = z_ref[rows, :].astype(F32)
        y = y * (zz * (1.0 / (1.0 + jnp.exp(-zz))))
        ms = jnp.mean(y * y, axis=-1, keepdims=True)
        o_ref[rows, :] = (y * lax.rsqrt(ms + NORM_EPS) * ng_ref[...]).astype(BF16)
        return carry

    lax.fori_loop(0, nc, bwd, 0)


def _ssd_call(proj, conv_w, conv_b, cum, rowr, wv, ev, dch, d_skip, norm_g, bsz, seq):
    t = proj.shape[0]
    g_n = SSM_GROUPS
    q = SSM_CHUNK
    gw = GROUP_WIDTH
    col_spec = pl.BlockSpec((1, 1, seq, DT_COLS), lambda b, g: (b, g, 0, 0))
    width = GROUP_WIDTH + 2 * SSM_STATE
    return pl.pallas_call(
        _ssd_kernel,
        grid=(bsz, g_n),
        in_specs=[
            pl.BlockSpec((1, 1, seq // q, DT_COLS), lambda b, g: (b, g, 0, 0), memory_space=pltpu.SMEM),
            pl.BlockSpec((seq, gw), lambda b, g: (b, COL_XS // gw + g)),
            pl.BlockSpec((seq, SSM_STATE), lambda b, g: (b, COL_B // SSM_STATE + g)),
            pl.BlockSpec((seq, SSM_STATE), lambda b, g: (b, COL_C // SSM_STATE + g)),
            pl.BlockSpec((seq, gw), lambda b, g: (b, COL_Z // gw + g)),
            pl.BlockSpec((SSM_CONV, gw), lambda b, g: (0, g)),
            pl.BlockSpec((SSM_CONV, SSM_STATE), lambda b, g: (0, SSM_INNER // SSM_STATE + g)),
            pl.BlockSpec((SSM_CONV, SSM_STATE), lambda b, g: (0, SSM_INNER // SSM_STATE + g_n + g)),
            pl.BlockSpec((1, gw), lambda b, g: (0, g)),
            pl.BlockSpec((1, SSM_STATE), lambda b, g: (0, SSM_INNER // SSM_STATE + g)),
            pl.BlockSpec((1, SSM_STATE), lambda b, g: (0, SSM_INNER // SSM_STATE + g_n + g)),
            col_spec,
            pl.BlockSpec((1, 1, DT_COLS, seq), lambda b, g: (b, g, 0, 0)),
            col_spec,
            col_spec,
            pl.BlockSpec((1, gw), lambda b, g: (0, g)),
            pl.BlockSpec((1, gw), lambda b, g: (0, g)),
        ],
        out_specs=pl.BlockSpec((seq, gw), lambda b, g: (b, g)),
        out_shape=jax.ShapeDtypeStruct((t, SSM_INNER), BF16),
        scratch_shapes=[
            pltpu.VMEM((seq + 32, width), F32),
            pltpu.VMEM((seq, gw), F32),
            pltpu.VMEM((seq, SSM_STATE), F32),
            pltpu.VMEM((seq, SSM_STATE), F32),
            pltpu.VMEM((seq, gw), F32),
            pltpu.VMEM((SSM_STATE, gw), F32),
        ],
        compiler_params=_params("arbitrary", "arbitrary"),
        name="ssd_mixer",
    )(dch, proj, proj, proj, proj, conv_w, conv_w, conv_w, conv_b, conv_b, conv_b,
      cum, rowr, wv, ev, d_skip, norm_g)


def _outproj_kernel(attn_ref, ssm_ref, ga_ref, gs_ref, x_ref, mod_ref, wa_ref, ws_ref, wo_ref, o_ref):
    ya = jnp.dot(attn_ref[...], wa_ref[...], preferred_element_type=F32)
    ys = jnp.dot(ssm_ref[...], ws_ref[...], preferred_element_type=F32)
    ga = ga_ref[...].astype(F32)
    gs = gs_ref[...].astype(F32)
    merged = ya * (1.0 / (1.0 + jnp.exp(-ga))) + ys * (1.0 / (1.0 + jnp.exp(-gs)))
    y = jnp.dot(merged.astype(BF16), wo_ref[...], preferred_element_type=F32)
    o_ref[...] = x_ref[...] + mod_ref[0, 2:3, :] * y


def _outproj_call(attn, ssm, proj, x2, mod, w_attn_o, w_ssm_o, w_out, seq):
    t, d = x2.shape
    tm = 512
    per_batch = seq // tm
    const = lambda i: (0, 0)
    return pl.pallas_call(
        _outproj_kernel,
        grid=(t // tm,),
        in_specs=[
            pl.BlockSpec((tm, ATTN_WIDTH), lambda i: (i, 0)),
            pl.BlockSpec((tm, SSM_INNER), lambda i: (i, 0)),
            pl.BlockSpec((tm, d), lambda i: (i, COL_GATE_ATTN // d)),
            pl.BlockSpec((tm, d), lambda i: (i, COL_GATE_SSM // d)),
            pl.BlockSpec((tm, d), lambda i: (i, 0)),
            pl.BlockSpec((1, N_MOD, d), lambda i: (i // per_batch, 0, 0)),
            pl.BlockSpec((ATTN_WIDTH, d), const),
            pl.BlockSpec((SSM_INNER, d), const),
            pl.BlockSpec((d, d), const),
        ],
        out_specs=pl.BlockSpec((tm, d), lambda i: (i, 0)),
        out_shape=jax.ShapeDtypeStruct((t, d), F32),
        compiler_params=_params("arbitrary"),
        name="out_proj",
    )(attn, ssm, proj, proj, x2, mod, w_attn_o, w_ssm_o, w_out)


def _permute_in_proj(w_in):
    w_main = jnp.concatenate(
        [w_in[:, _O_Q:_O_K], w_in[:, _O_GATES:], w_in[:, _O_Z:_O_XBC], w_in[:, _O_XBC:_O_XBC + SSM_INNER],
         w_in[:, _O_K:_O_Z], w_in[:, _O_XBC + SSM_INNER:_O_DT]], axis=1).astype(BF16)
    return w_main


def _dt_permutation():
    idx = np.zeros(2 * SSM_HEADS, np.int32)
    for g in range(SSM_GROUPS):
        for d in range(2):
            for r in range(SSM_HPG):
                idx[g * DT_COLS + d * SSM_HPG + r] = d * SSM_HEADS + g * SSM_HPG + r
    return idx


def _pad_lanes(a):
    return jnp.pad(a, [(0, 0)] * (a.ndim - 1) + [(0, LANES - a.shape[-1])])


def _mixer_layer(x2, mod, cos_t, sin_t, bsz, seq, norm_g, w_in, q_norm_g, k_norm_g, attn_sink, conv_w, conv_b,
                 a_log, dt_bias, ssm_d, ssm_norm_g, w_attn_o, w_ssm_o, w_out):
    d = x2.shape[1]
    gain = norm_g.reshape(1, d)
    perm = _dt_permutation()
    w_dt = _pad_lanes(w_in[:, _O_DT:_O_GATES][:, perm])
    dt_b = _pad_lanes(dt_bias.reshape(-1)[perm].reshape(1, -1))
    a_lg = _pad_lanes(a_log.reshape(-1)[perm].reshape(1, -1))

    proj = _inproj_call(x2, mod, gain, _permute_in_proj(w_in), seq)
    cum, rowr, wv, ev, dch = _dt_call(x2, mod, gain, w_dt, dt_b, a_lg, bsz, seq)
    nc = seq // SSM_CHUNK
    dch = dch[:, :, 0, :2 * SSM_HEADS].reshape(bsz, nc, SSM_GROUPS, DT_COLS).transpose(0, 2, 1, 3)

    attn = _attn_call(proj, cos_t, sin_t, jnp.tile(q_norm_g, 2).reshape(1, LANES),
                      jnp.tile(k_norm_g, 2).reshape(1, LANES), attn_sink, bsz, seq)
    ssm = _ssd_call(proj, conv_w, conv_b.reshape(1, -1), cum, rowr, wv, ev, dch,
                    jnp.repeat(ssm_d, SSM_HEAD_DIM).reshape(1, SSM_INNER), ssm_norm_g.reshape(1, SSM_INNER),
                    bsz, seq)
    return _outproj_call(attn, ssm, proj, x2, mod, w_attn_o.astype(BF16), w_ssm_o.astype(BF16),
                         w_out.astype(BF16), seq)


ROUTE_BLOCK = 256
SLAB = SUBLANES
LOCAL_ROWS = ROUTE_BLOCK * TOP_K + ROUTE_BLOCK
LOCAL_SLABS = LOCAL_ROWS // SLAB
TILE_ROWS = 256
TILE_SLABS = TILE_ROWS // SLAB


def _route_kernel(x_ref, mod_ref, g_ref, rw_ref, rb_ref, xloc_ref, meta_ref, cnt_ref):
    tb = ROUTE_BLOCK
    h = _prenorm_modulate(x_ref[...], g_ref[...], mod_ref[0, 3:4, :], mod_ref[0, 4:5, :])
    logits = jnp.dot(h, rw_ref[...], precision=HIGHEST, preferred_element_type=F32) + rb_ref[...]
    v = logits.T[0:N_EXPERTS, :]
    erow = lax.broadcasted_iota(jnp.int32, (N_EXPERTS, tb), 0)
    hots, tops = [], []
    for _ in range(TOP_K):
        mk = jnp.max(v, axis=0, keepdims=True)
        first = jnp.min(jnp.where(v == mk, erow, N_EXPERTS), axis=0, keepdims=True)
        hot = erow == first
        v = jnp.where(hot, -jnp.inf, v)
        hots.append(hot)
        tops.append(mk)
    ps = [jnp.exp(mk - tops[0]) for mk in tops]
    denom = ps[0] + ps[1] + ps[2] + ps[3]
    sel = jnp.zeros((N_EXPERTS, tb), F32)
    for hot in hots:
        sel = sel + jnp.where(hot, 1.0, 0.0)
    ti = lax.broadcasted_iota(jnp.int32, (tb, tb), 0)
    tj = lax.broadcasted_iota(jnp.int32, (tb, tb), 1)
    before = jnp.where(ti < tj, 1.0, 0.0).astype(BF16)
    rank = jnp.dot(sel.astype(BF16), before, preferred_element_type=F32)
    count = jnp.sum(sel, axis=1, keepdims=True)
    slabs = jnp.floor((count + (SLAB - 1)) * (1.0 / SLAB))
    ei = lax.broadcasted_iota(jnp.int32, (N_EXPERTS, N_EXPERTS), 0)
    ej = lax.broadcasted_iota(jnp.int32, (N_EXPERTS, N_EXPERTS), 1)
    earlier = jnp.where(ej < ei, 1.0, 0.0).astype(BF16)
    slabs_b = jnp.broadcast_to(slabs, (N_EXPERTS, tb))
    start = jnp.dot(earlier, slabs_b.astype(BF16), preferred_element_type=F32)
    dest = start * SLAB + rank
    dks = [jnp.sum(jnp.where(hot, dest, 0.0), axis=0, keepdims=True) for hot in hots]
    meta_ref[0] = jnp.concatenate(dks + [p / denom for p in ps], axis=0)
    cnt_ref[0] = slabs_b[:, 0:LANES]
    hb = h.astype(BF16)
    dki = [dk.astype(jnp.int32) for dk in dks]
    for c in range(LOCAL_ROWS // tb):
        ri = lax.broadcasted_iota(jnp.int32, (tb, tb), 0) + c * tb
        onehot = jnp.where((ri == dki[0]) | (ri == dki[1]) | (ri == dki[2]) | (ri == dki[3]), 1.0, 0.0)
        xloc_ref[0, c * tb:(c + 1) * tb, :] = jnp.dot(onehot.astype(BF16), hb, preferred_element_type=F32)


def _route_call(x2, mod, gain, router_w, router_b, seq):
    t, d = x2.shape
    tb = ROUTE_BLOCK
    nblk = t // tb
    per_batch = seq // tb
    return pl.pallas_call(
        _route_kernel,
        grid=(nblk,),
        in_specs=[
            pl.BlockSpec((tb, d), lambda i: (i, 0)),
            pl.BlockSpec((1, N_MOD, d), lambda i: (i // per_batch, 0, 0)),
            pl.BlockSpec((1, d), lambda i: (0, 0)),
            pl.BlockSpec((d, LANES), lambda i: (0, 0)),
            pl.BlockSpec((1, LANES), lambda i: (0, 0)),
        ],
        out_specs=[
            pl.BlockSpec((1, LOCAL_ROWS, d), lambda i: (i, 0, 0)),
            pl.BlockSpec((1, 2 * TOP_K, tb), lambda i: (i, 0, 0)),
            pl.BlockSpec((1, N_EXPERTS, LANES), lambda i: (i, 0, 0)),
        ],
        out_shape=[
            jax.ShapeDtypeStruct((nblk, LOCAL_ROWS, d), F32),
            jax.ShapeDtypeStruct((nblk, 2 * TOP_K, tb), F32),
            jax.ShapeDtypeStruct((nblk, N_EXPERTS, LANES), F32),
        ],
        compiler_params=_params("arbitrary"),
        name="moe_route",
    )(x2, mod, gain, _pad_lanes(router_w), _pad_lanes(router_b.reshape(1, -1)))


def _slab_plan(slab_counts):
    nblk = slab_counts.shape[0]
    max_slabs = nblk * LOCAL_SLABS + N_EXPERTS * (TILE_SLABS - 1)
    max_tiles = -(-max_slabs // TILE_SLABS)
    c8 = slab_counts.astype(jnp.int32)
    local_start = jnp.cumsum(c8, axis=1) - c8
    per_expert = jnp.sum(c8, axis=0)
    tiles_e = (per_expert + TILE_SLABS - 1) // TILE_SLABS
    tile_start = jnp.cumsum(tiles_e) - tiles_e
    num_tiles = jnp.sum(tiles_e)
    expert_start = tile_start * TILE_SLABS
    block_off = jnp.cumsum(c8, axis=0) - c8
    seg_start = expert_start[None, :] + block_off

    sl = jnp.arange(LOCAL_SLABS, dtype=jnp.int32)
    e_of = jnp.sum(local_start[:, None, :] <= sl[None, :, None], axis=-1) - 1
    used = sl[None, :] < (local_start[:, -1] + c8[:, -1])[:, None]
    g_of = jnp.take_along_axis(seg_start, e_of, axis=1) + sl[None, :] - jnp.take_along_axis(local_start, e_of, axis=1)
    slab_pos = jnp.where(used, g_of, 0).reshape(-1)

    p = jnp.arange(max_tiles * TILE_SLABS, dtype=jnp.int32)
    e_p = jnp.clip(jnp.sum(expert_start[None, :] <= p[:, None], axis=-1) - 1, 0, N_EXPERTS - 1)
    off = p - expert_start[e_p]
    boff_e = block_off.T[e_p]
    b_p = jnp.clip(jnp.sum(boff_e <= off[:, None], axis=-1) - 1, 0, nblk - 1)
    valid = off < per_expert[e_p]
    src = b_p * LOCAL_SLABS + local_start[b_p, e_p] + off - block_off[b_p, e_p]
    slab_src = jnp.where(valid, src, 0)

    ti = jnp.arange(max_tiles, dtype=jnp.int32)
    tile_expert = jnp.clip(jnp.sum(tile_start[None, :] <= jnp.minimum(ti, num_tiles - 1)[:, None], axis=-1) - 1,
                           0, N_EXPERTS - 1)
    return tile_expert.astype(jnp.int32), slab_src.astype(jnp.int32), slab_pos.astype(jnp.int32), \
        num_tiles.reshape(1).astype(jnp.int32), max_tiles


def _expert_kernel(te_ref, src_ref, nt_ref, xloc_hbm, wg_ref, bg_ref, wu_ref, bu_ref, wd_ref, bd_ref, y_ref,
                   xbuf, sem, wg_s, wu_s, wd_s):
    i = pl.program_id(0)
    nt = nt_ref[0]

    def slab_copy(tile, slot, j):
        return pltpu.make_async_copy(xloc_hbm.at[src_ref[tile * TILE_SLABS + j]], xbuf.at[slot, j], sem.at[slot])

    def issue(tile, slot):
        for j in range(TILE_SLABS):
            slab_copy(tile, slot, j).start()

    @pl.when(i == 0)
    def _():
        issue(0, 0)

    @pl.when(i + 1 < nt)
    def _():
        issue(i + 1, (i + 1) % 2)

    @pl.when(i < nt)
    def _():
        slot = i % 2
        for j in range(TILE_SLABS):
            slab_copy(i, slot, j).wait()
        new_expert = jnp.logical_or(i == 0, te_ref[i] != te_ref[jnp.maximum(i - 1, 0)])

        @pl.when(new_expert)
        def _():
            rows = 128
            for src, dst in ((wg_ref, wg_s), (wu_ref, wu_s), (wd_ref, wd_s)):
                def cast(c, carry, src=src, dst=dst):
                    r = pl.ds(pl.multiple_of(c * rows, rows), rows)
                    dst[r, :] = src[0, r, :].astype(BF16)
                    return carry
                lax.fori_loop(0, src.shape[1] // rows, cast, 0)

        x = xbuf[slot].reshape(TILE_ROWS, xbuf.shape[-1]).astype(BF16)
        gate = jnp.dot(x, wg_s[...], preferred_element_type=F32) + bg_ref[0]
        up = jnp.dot(x, wu_s[...], preferred_element_type=F32) + bu_ref[0]
        glu = jnp.minimum(gate, SWIGLU_LIMIT)
        lin = jnp.clip(up, -SWIGLU_LIMIT, SWIGLU_LIMIT)
        act = glu * (1.0 / (1.0 + jnp.exp(-SWIGLU_ALPHA * glu))) * (lin + 1.0)
        y_ref[...] = jnp.dot(act.astype(BF16), wd_s[...], preferred_element_type=F32) + bd_ref[0]

    @pl.when(i >= nt)
    def _():
        y_ref[...] = jnp.zeros_like(y_ref)


def _expert_call(tile_expert, slab_src, num_tiles, max_tiles, xloc, w_gate, b_gate, w_up, b_up, w_down, b_down):
    nblk, _, d = xloc.shape
    ff = w_gate.shape[2]
    xloc3 = xloc.reshape(nblk * LOCAL_SLABS, SLAB, d)
    wspec = lambda k, n: pl.BlockSpec((1, k, n), lambda i, te, ss, nt: (te[i], 0, 0))
    return pl.pallas_call(
        _expert_kernel,
        grid_spec=pltpu.PrefetchScalarGridSpec(
            num_scalar_prefetch=3,
            grid=(max_tiles,),
            in_specs=[
                pl.BlockSpec(memory_space=pl.ANY),
                wspec(d, ff), wspec(1, ff), wspec(d, ff), wspec(1, ff), wspec(ff, d), wspec(1, d),
            ],
            out_specs=pl.BlockSpec((TILE_ROWS, d), lambda i, te, ss, nt: (i, 0)),
            scratch_shapes=[
                pltpu.VMEM((2, TILE_SLABS, SLAB, d), F32),
                pltpu.SemaphoreType.DMA((2,)),
                pltpu.VMEM((d, ff), BF16),
                pltpu.VMEM((d, ff), BF16),
                pltpu.VMEM((ff, d), BF16),
            ],
        ),
        out_shape=jax.ShapeDtypeStruct((max_tiles * TILE_ROWS, d), F32),
        compiler_params=_params("arbitrary"),
        name="moe_experts",
    )(tile_expert, slab_src, num_tiles, xloc3, w_gate, b_gate.reshape(N_EXPERTS, 1, ff), w_up,
      b_up.reshape(N_EXPERTS, 1, ff), w_down, b_down.reshape(N_EXPERTS, 1, d))


def _combine_kernel(pos_ref, y_hbm, meta_ref, x_ref, mod_ref, o_ref, ybuf, sem):
    b = pl.program_id(0)
    nb = pl.num_programs(0)
    tb = ROUTE_BLOCK

    def slab_copy(blk, slot, j):
        return pltpu.make_async_copy(y_hbm.at[pos_ref[blk * LOCAL_SLABS + j]], ybuf.at[slot, j], sem.at[slot])

    def issue(blk, slot):
        def body(j, carry):
            slab_copy(blk, slot, j).start()
            return carry
        lax.fori_loop(0, LOCAL_SLABS, body, 0)

    @pl.when(b == 0)
    def _():
        issue(0, 0)

    @pl.when(b + 1 < nb)
    def _():
        issue(b + 1, (b + 1) % 2)

    slot = b % 2

    def wait_body(j, carry):
        slab_copy(b, slot, j).wait()
        return carry
    lax.fori_loop(0, LOCAL_SLABS, wait_body, 0)

    meta = meta_ref[0]
    dki = [meta[k:k + 1, :].astype(jnp.int32) for k in range(TOP_K)]
    wks = [meta[TOP_K + k:TOP_K + k + 1, :] for k in range(TOP_K)]
    acc = jnp.zeros((tb, x_ref.shape[-1]), F32)
    slabs_per_chunk = tb // SLAB
    for c in range(LOCAL_ROWS // tb):
        ri = lax.broadcasted_iota(jnp.int32, (tb, tb), 0) + c * tb
        hits = [ri == dk for dk in dki]
        onehot = jnp.where(hits[0] | hits[1] | hits[2] | hits[3], 1.0, 0.0)
        weighted = jnp.where(hits[0], wks[0], jnp.where(hits[1], wks[1], jnp.where(hits[2], wks[2],
                             jnp.where(hits[3], wks[3], 0.0))))
        w_row = jnp.sum(weighted, axis=1, keepdims=True)
        y = ybuf[slot, c * slabs_per_chunk:(c + 1) * slabs_per_chunk].reshape(tb, x_ref.shape[-1])
        yw = (y * w_row).astype(BF16)
        acc = acc + lax.dot_general(onehot.astype(BF16), yw, (((0,), (0,)), ((), ())), preferred_element_type=F32)
    o_ref[...] = x_ref[...] + mod_ref[0, 5:6, :] * acc


def _combine_call(slab_pos, y, meta, x2, mod, seq):
    t, d = x2.shape
    tb = ROUTE_BLOCK
    nblk = t // tb
    per_batch = seq // tb
    y3 = y.reshape(y.shape[0] // SLAB, SLAB, d)
    return pl.pallas_call(
        _combine_kernel,
        grid_spec=pltpu.PrefetchScalarGridSpec(
            num_scalar_prefetch=1,
            grid=(nblk,),
            in_specs=[
                pl.BlockSpec(memory_space=pl.ANY),
                pl.BlockSpec((1, 2 * TOP_K, tb), lambda i, sp: (i, 0, 0)),
                pl.BlockSpec((tb, d), lambda i, sp: (i, 0)),
                pl.BlockSpec((1, N_MOD, d), lambda i, sp: (i // per_batch, 0, 0)),
            ],
            out_specs=pl.BlockSpec((tb, d), lambda i, sp: (i, 0)),
            scratch_shapes=[
                pltpu.VMEM((2, LOCAL_SLABS, SLAB, d), F32),
                pltpu.SemaphoreType.DMA((2,)),
            ],
        ),
        out_shape=jax.ShapeDtypeStruct((t, d), F32),
        compiler_params=_params("arbitrary"),
        name="moe_combine",
    )(slab_pos, y3, meta, x2, mod)


def _moe_layer(x2, mod, seq, norm_g, router_w, router_b, w_gate, b_gate, w_up, b_up, w_down, b_down):
    d = x2.shape[1]
    xloc, meta, cnt = _route_call(x2, mod, norm_g.reshape(1, d), router_w, router_b, seq)
    tile_expert, slab_src, slab_pos, num_tiles, max_tiles = _slab_plan(cnt[:, :, 0])
    y = _expert_call(tile_expert, slab_src, num_tiles, max_tiles, xloc, w_gate, b_gate, w_up, b_up, w_down, b_down)
    return _combine_call(slab_pos, y, meta, x2, mod, seq)


def kernel(x, c, positions, ada_w, ada_b, norm1_g, norm2_g, w_in, q_norm_g, k_norm_g, attn_sink, conv_w, conv_b,
           a_log, dt_bias, ssm_d, ssm_norm_g, w_attn_o, w_ssm_o, w_out, router_w, router_b, exp_w_gate, exp_b_gate,
           exp_w_up, exp_b_up, exp_w_down, exp_b_down):
    bsz, seq, d = x.shape
    depth = ada_w.shape[0]
    assert d == D_MODEL and seq % 512 == 0 and (bsz * seq) % ROUTE_BLOCK == 0
    mod_all = _ada_call(c, ada_w, ada_b).reshape(depth, bsz, N_MOD, d)
    cos_t, sin_t = _rope_call(positions)
    x2 = x.reshape(bsz * seq, d)
    for l in range(depth):
        mod = mod_all[l]
        x2 = _mixer_layer(x2, mod, cos_t, sin_t, bsz, seq, norm1_g[l], w_in[l], q_norm_g[l], k_norm_g[l],
                          attn_sink[l], conv_w[l], conv_b[l], a_log[l], dt_bias[l], ssm_d[l], ssm_norm_g[l],
                          w_attn_o[l], w_ssm_o[l], w_out[l])
        x2 = _moe_layer(x2, mod, seq, norm2_g[l], router_w[l], router_b[l], exp_w_gate[l], exp_b_gate[l],
                        exp_w_up[l], exp_b_up[l], exp_w_down[l], exp_b_down[l])
    return x2.reshape(bsz, seq, d)
```

```python
import functools

import jax
import jax.numpy as jnp
import numpy as np
from jax import lax
from jax.experimental import pallas as pl
from jax.experimental.pallas import tpu as pltpu

F32 = jnp.float32
BF16 = jnp.bfloat16
HIGHEST = lax.Precision.HIGHEST

LANES = 128
SUBLANES = 8
VMEM_LIMIT = 56 * 1024 * 1024

D_MODEL = 1024
NORM_EPS = 1e-5
N_MOD = 6

HEAD_DIM = 64
N_Q_HEADS = 16
N_KV_HEADS = 4
ATTN_WIDTH = N_Q_HEADS * HEAD_DIM
KV_WIDTH = N_KV_HEADS * HEAD_DIM
WINDOW = 128
ATTN_BLOCK = 128
ROPE_THETA = 500000.0
ROPE_DIMS = HEAD_DIM // 4
ROPE_HALF = ROPE_DIMS // 2
NEG_BIG = -1e30

SSM_INNER = 2 * D_MODEL
SSM_HEAD_DIM = 64
SSM_HEADS = SSM_INNER // SSM_HEAD_DIM
SSM_GROUPS = 4
SSM_HPG = SSM_HEADS // SSM_GROUPS
SSM_STATE = 128
SSM_CONV = 5
SSM_CHUNK = 128
GROUP_WIDTH = SSM_INNER // SSM_GROUPS
DT_COLS = 2 * SSM_HPG

N_EXPERTS = 32
TOP_K = 4
SWIGLU_LIMIT = 7.0
SWIGLU_ALPHA = 1.702

COL_Q = 0
COL_GATE_ATTN = 1024
COL_GATE_SSM = 2048
COL_Z = 3072
COL_XS = 5120
COL_K = 7168
COL_V = 7424
COL_B = 7680
COL_C = 8192
PROJ_WIDTH = 8704
PROJ_N_TILE = PROJ_WIDTH // 4

_O_Q, _O_K, _O_V, _O_Z, _O_XBC, _O_DT, _O_GATES = 0, 1024, 1280, 1536, 3584, 6656, 6720


def _params(*sem):
    return pltpu.CompilerParams(dimension_semantics=sem, vmem_limit_bytes=VMEM_LIMIT)


def _prenorm_modulate(x, gain, shift, scale):
    ms = jnp.mean(x * x, axis=-1, keepdims=True)
    return (x * lax.rsqrt(ms + NORM_EPS) * gain) * (1.0 + scale) + shift


def _ada_kernel(c_ref, w_ref, b_ref, o_ref):
    c = c_ref[...]
    c_act = c * (1.0 / (1.0 + jnp.exp(-c)))
    o_ref[0] = jnp.dot(c_act, w_ref[0], precision=HIGHEST, preferred_element_type=F32) + b_ref[0]


def _ada_call(c, ada_w, ada_b):
    depth, d, n = ada_w.shape
    bsz = c.shape[0]
    tn = 1536
    return pl.pallas_call(
        _ada_kernel,
        grid=(depth, n // tn),
        in_specs=[
            pl.BlockSpec((bsz, d), lambda l, j: (0, 0)),
            pl.BlockSpec((1, d, tn), lambda l, j: (l, 0, j)),
            pl.BlockSpec((1, 1, tn), lambda l, j: (l, 0, j)),
        ],
        out_specs=pl.BlockSpec((1, bsz, tn), lambda l, j: (l, 0, j)),
        out_shape=jax.ShapeDtypeStruct((depth, bsz, n), F32),
        compiler_params=_params("arbitrary", "arbitrary"),
        name="ada_mod",
    )(c, ada_w, ada_b.reshape(depth, 1, n))


def _rope_kernel(pos_ref, freq_ref, sign_ref, cos_ref, sin_ref):
    ang = pos_ref[0].astype(F32) * freq_ref[...]
    sign = sign_ref[...]
    cos_ref[0] = jnp.where(sign == 0.0, 1.0, jnp.cos(ang))
    sin_ref[0] = jnp.sin(ang) * sign


def _rope_call(positions):
    bsz, s = positions.shape
    lane = np.arange(LANES) % HEAD_DIM
    inv_freq = ROPE_THETA ** (-np.arange(0, ROPE_DIMS, 2, dtype=np.float32) / ROPE_DIMS)
    freq = np.where(lane < ROPE_DIMS, inv_freq[lane % ROPE_HALF], 0.0).astype(np.float32)
    sign = np.where(lane < ROPE_HALF, -1.0, np.where(lane < ROPE_DIMS, 1.0, 0.0)).astype(np.float32)
    ts = 512
    spec = pl.BlockSpec((1, ts, LANES), lambda b, i: (b, i, 0))
    return pl.pallas_call(
        _rope_kernel,
        grid=(bsz, s // ts),
        in_specs=[
            pl.BlockSpec((1, ts, 1), lambda b, i: (b, i, 0)),
            pl.BlockSpec((1, LANES), lambda b, i: (0, 0)),
            pl.BlockSpec((1, LANES), lambda b, i: (0, 0)),
        ],
        out_specs=[spec, spec],
        out_shape=[jax.ShapeDtypeStruct((bsz, s, LANES), F32)] * 2,
        compiler_params=_params("arbitrary", "arbitrary"),
        name="rope_tables",
    )(positions.reshape(bsz, s, 1), jnp.asarray(freq).reshape(1, LANES), jnp.asarray(sign).reshape(1, LANES))


def _inproj_kernel(x_ref, mod_ref, g_ref, w_ref, o_ref):
    h = _prenorm_modulate(x_ref[...], g_ref[...], mod_ref[0, 0:1, :], mod_ref[0, 1:2, :])
    o_ref[...] = jnp.dot(h.astype(BF16), w_ref[...], preferred_element_type=F32).astype(BF16)


def _inproj_call(x2, mod, gain, w_main, seq):
    t, d = x2.shape
    tm = 512
    per_batch = seq // tm
    return pl.pallas_call(
        _inproj_kernel,
        grid=(PROJ_WIDTH // PROJ_N_TILE, t // tm),
        in_specs=[
            pl.BlockSpec((tm, d), lambda n, i: (i, 0)),
            pl.BlockSpec((1, N_MOD, d), lambda n, i: (i // per_batch, 0, 0)),
            pl.BlockSpec((1, d), lambda n, i: (0, 0)),
            pl.BlockSpec((d, PROJ_N_TILE), lambda n, i: (0, n)),
        ],
        out_specs=pl.BlockSpec((tm, PROJ_N_TILE), lambda n, i: (i, n)),
        out_shape=jax.ShapeDtypeStruct((t, PROJ_WIDTH), BF16),
        compiler_params=_params("arbitrary", "arbitrary"),
        name="in_proj",
    )(x2, mod, gain, w_main)


PARTS = 3
LR_STRIDE = 8
DT_ROWS = 2 * PARTS
X_BLOCKS = ("ev_fwd", "wv_fwd", "ev_bwd", "wv_bwd")
X_STRIDE = PARTS * SSM_HPG
LOG_DT_FLOOR = -200.0
GW_ALL = SSM_GROUPS * LANES


def _placement_constants():
    place = np.zeros((5 * PARTS, LANES, GW_ALL), np.float32)
    ones = np.zeros((2, 1, GW_ALL), np.float32)
    for g in range(SSM_GROUPS):
        for j in range(DT_COLS):
            src = g * DT_COLS + j
            base = g * LANES + j * LR_STRIDE
            for k in range(PARTS):
                place[k, src, base + k] = 1.0
                place[PARTS + k, src, base + PARTS + k] = -1.0
            place[2 * PARTS, src, base + DT_ROWS] = 1.0
            place[2 * PARTS + 1, src, base + DT_ROWS + 1] = 1.0
            ones[0, 0, base + PARTS:base + 2 * PARTS] = 1.0
            ones[1, 0, base:base + PARTS] = 1.0
        for r in range(SSM_HPG):
            fwd, bwd = g * DT_COLS + r, g * DT_COLS + SSM_HPG + r
            for k in range(PARTS):
                col = g * LANES + k * SSM_HPG + r
                place[3 * PARTS + k, fwd, col + 0 * X_STRIDE] = 1.0
                place[3 * PARTS + k, bwd, col + 2 * X_STRIDE] = 1.0
                place[4 * PARTS + k, fwd, col + 1 * X_STRIDE] = 1.0
                place[4 * PARTS + k, bwd, col + 3 * X_STRIDE] = 1.0
    return jnp.asarray(place, BF16), jnp.asarray(ones, F32)


def _expansion_constant():
    ex = np.zeros((LANES, len(X_BLOCKS) * GROUP_WIDTH), np.float32)
    for t in range(len(X_BLOCKS)):
        for k in range(PARTS):
            for r in range(SSM_HPG):
                row = t * X_STRIDE + k * SSM_HPG + r
                ex[row, t * GROUP_WIDTH + r * SSM_HEAD_DIM:t * GROUP_WIDTH + (r + 1) * SSM_HEAD_DIM] = 1.0
    return jnp.asarray(ex, BF16)


def _split3(x):
    hi = x.astype(BF16)
    r1 = x - hi.astype(F32)
    mid = r1.astype(BF16)
    lo = (r1 - mid.astype(F32)).astype(BF16)
    return hi, mid, lo


def _dt_kernel(x_ref, mod_ref, g_ref, w_ref, bias_ref, alog_ref, place_ref, ones_ref,
               lmat_ref, rmat_ref, xmat_ref, dch_ref):
    q = SSM_CHUNK
    h = _prenorm_modulate(x_ref[...], g_ref[...], mod_ref[0, 0:1, :], mod_ref[0, 1:2, :])
    raw = jnp.dot(h, w_ref[...], precision=HIGHEST, preferred_element_type=F32) + bias_ref[...]
    dt = jnp.maximum(raw, 0.0) + jnp.log1p(jnp.exp(-jnp.abs(raw)))
    da = dt * (-jnp.exp(alog_ref[...]))
    row = lax.broadcasted_iota(jnp.int32, (q, q), 0)
    col = lax.broadcasted_iota(jnp.int32, (q, q), 1)
    prefix = jnp.dot((row >= col).astype(F32), da, precision=HIGHEST, preferred_element_type=F32)
    suffix = jnp.dot((row <= col).astype(F32), da, precision=HIGHEST, preferred_element_type=F32)
    lane = lax.broadcasted_iota(jnp.int32, (1, LANES), 1)
    is_fwd = (lane % DT_COLS) < SSM_HPG
    cum = jnp.where(is_fwd, prefix, suffix)
    tot = jnp.where(is_fwd, prefix[q - 1:q, :], suffix[0:1, :])
    ev = jnp.exp(cum)
    wv = jnp.exp(tot - cum) * dt
    rsub = cum - jnp.maximum(jnp.log(dt), LOG_DT_FLOOR)

    def placed(first, *values):
        acc = None
        for i, v in enumerate(values):
            for k, part in enumerate(_split3(v)):
                term = jnp.dot(part, place_ref[first + i * PARTS + k], preferred_element_type=F32)
                acc = term if acc is None else acc + term
        return acc

    lmat_ref[...] = (placed(0, cum) + ones_ref[0]).astype(BF16)
    rmat_ref[...] = (placed(PARTS, rsub) + placed(2 * PARTS, dt) + ones_ref[1]).T.astype(BF16)
    xmat_ref[...] = placed(3 * PARTS, ev, wv).astype(BF16)
    dch_ref[0, 0] = jnp.exp(tot)


def _dt_call(x2, mod, gain, w_dt, dt_bias, a_log, bsz, seq):
    t, d = x2.shape
    q = SSM_CHUNK
    nc = seq // q
    place, ones = _placement_constants()
    return pl.pallas_call(
        _dt_kernel,
        grid=(bsz, nc),
        in_specs=[
            pl.BlockSpec((q, d), lambda b, c: (b * nc + c, 0)),
            pl.BlockSpec((1, N_MOD, d), lambda b, c: (b, 0, 0)),
            pl.BlockSpec((1, d), lambda b, c: (0, 0)),
            pl.BlockSpec((d, LANES), lambda b, c: (0, 0)),
            pl.BlockSpec((1, LANES), lambda b, c: (0, 0)),
            pl.BlockSpec((1, LANES), lambda b, c: (0, 0)),
            pl.BlockSpec(place.shape, lambda b, c: (0, 0, 0)),
            pl.BlockSpec(ones.shape, lambda b, c: (0, 0, 0)),
        ],
        out_specs=[
            pl.BlockSpec((q, GW_ALL), lambda b, c: (b * nc + c, 0)),
            pl.BlockSpec((GW_ALL, q), lambda b, c: (b, c)),
            pl.BlockSpec((q, GW_ALL), lambda b, c: (b * nc + c, 0)),
            pl.BlockSpec((1, 1, 1, LANES), lambda b, c: (b, c, 0, 0)),
        ],
        out_shape=[
            jax.ShapeDtypeStruct((t, GW_ALL), BF16),
            jax.ShapeDtypeStruct((bsz * GW_ALL, seq), BF16),
            jax.ShapeDtypeStruct((t, GW_ALL), BF16),
            jax.ShapeDtypeStruct((bsz, nc, 1, LANES), F32),
        ],
        compiler_params=_params("arbitrary", "arbitrary"),
        name="dt_prep",
    )(x2, mod, gain, w_dt, dt_bias, a_log, place, ones)


def _head_rms_rope(t, gain, cos, sin, lane):
    lo = lane < HEAD_DIM
    sq = t * t
    ss_lo = jnp.sum(jnp.where(lo, sq, 0.0), axis=-1, keepdims=True)
    ss_hi = jnp.sum(jnp.where(lo, 0.0, sq), axis=-1, keepdims=True)
    r = jnp.where(lo, lax.rsqrt(ss_lo * (1.0 / HEAD_DIM) + NORM_EPS), lax.rsqrt(ss_hi * (1.0 / HEAD_DIM) + NORM_EPS))
    tn = t * r * gain
    first_half = (lane % HEAD_DIM) < ROPE_HALF
    partner = jnp.where(first_half, pltpu.roll(tn, LANES - ROPE_HALF, 1), pltpu.roll(tn, ROPE_HALF, 1))
    return tn * cos + partner * sin


def _attn_kernel(sink_ref, q_ref, k_ref, v_ref, cos_ref, sin_ref, qg_ref, kg_ref, o_ref, kpad, vpad):
    blk = ATTN_BLOCK
    s = q_ref.shape[0]
    nb = s // blk
    lane = lax.broadcasted_iota(jnp.int32, (1, LANES), 1)
    lo = lane < HEAD_DIM

    zero_blk = jnp.zeros((blk, LANES), BF16)
    for j in range(2 * N_KV_HEADS):
        kpad[j, 0:blk, :] = zero_blk
        kpad[j, blk + s:2 * blk + s, :] = zero_blk
        vpad[j, 0:blk, :] = zero_blk
        vpad[j, blk + s:2 * blk + s, :] = zero_blk

    def prep(c, carry):
        r0 = pl.multiple_of(c * blk, blk)
        rows = pl.ds(r0, blk)
        dst = pl.ds(r0 + blk, blk)
        cos = cos_ref[0, rows, :]
        sin = sin_ref[0, rows, :]
        for tpair in range(N_KV_HEADS // 2):
            ls = slice(tpair * LANES, (tpair + 1) * LANES)
            kr = _head_rms_rope(k_ref[rows, ls].astype(F32), kg_ref[...], cos, sin, lane)
            vv = v_ref[rows, ls].astype(F32)
            for src, store in ((kr, kpad), (vv, vpad)):
                even_lo = jnp.where(lo, src, 0.0)
                odd_hi = jnp.where(lo, 0.0, src)
                g0, g1 = 2 * tpair, 2 * tpair + 1
                store[2 * g0, dst, :] = even_lo.astype(BF16)
                store[2 * g0 + 1, dst, :] = pltpu.roll(even_lo, HEAD_DIM, 1).astype(BF16)
                store[2 * g1, dst, :] = pltpu.roll(odd_hi, HEAD_DIM, 1).astype(BF16)
                store[2 * g1 + 1, dst, :] = odd_hi.astype(BF16)
        return carry

    lax.fori_loop(0, nb, prep, 0)

    rowi = lax.broadcasted_iota(jnp.int32, (2 * blk, 3 * blk), 0) % blk
    coli = lax.broadcasted_iota(jnp.int32, (2 * blk, 3 * blk), 1)
    rel = coli - rowi
    in_band = (rel >= 0) & (rel <= 2 * WINDOW)
    top_rows = lax.broadcasted_iota(jnp.int32, (2 * blk, 1), 0) < blk
    scale = HEAD_DIM ** -0.5

    def qblock(n, carry):
        r0 = pl.multiple_of(n * blk, blk)
        rows = pl.ds(r0, blk)
        band = pl.ds(r0, 3 * blk)
        cos = cos_ref[0, rows, :]
        sin = sin_ref[0, rows, :]
        kidx = coli + (r0 - blk)
        bias = jnp.where(in_band & (kidx >= 0) & (kidx < s), 0.0, NEG_BIG)
        for g in range(N_KV_HEADS):
            qs = []
            for pair in (2 * g, 2 * g + 1):
                qt = q_ref[rows, pair * LANES:(pair + 1) * LANES].astype(F32)
                qs.append(_head_rms_rope(qt, qg_ref[...], cos, sin, lane) * scale)
            qq = jnp.concatenate(qs, axis=0).astype(BF16)
            acc = jnp.zeros((2 * blk, LANES), F32)
            for par in range(2):
                kb = kpad[2 * g + par, band, :]
                sc = lax.dot_general(qq, kb, (((1,), (1,)), ((), ())), preferred_element_type=F32) + bias
                sink = jnp.where(top_rows, sink_ref[4 * g + par], sink_ref[4 * g + 2 + par])
                m = jnp.maximum(jnp.max(sc, axis=-1, keepdims=True), sink)
                p = jnp.exp(sc - m)
                denom = jnp.sum(p, axis=-1, keepdims=True) + jnp.exp(sink - m)
                pv = jnp.dot(p.astype(BF16), vpad[2 * g + par, band, :], preferred_element_type=F32)
                acc = acc + pv * (1.0 / denom)
            o_ref[rows, (2 * g) * LANES:(2 * g + 1) * LANES] = acc[0:blk].astype(BF16)
            o_ref[rows, (2 * g + 1) * LANES:(2 * g + 2) * LANES] = acc[blk:2 * blk].astype(BF16)
        return carry

    lax.fori_loop(0, nb, qblock, 0)


def _attn_call(proj, cos_t, sin_t, q_gain, k_gain, sink, bsz, seq):
    t = proj.shape[0]
    return pl.pallas_call(
        _attn_kernel,
        grid_spec=pltpu.PrefetchScalarGridSpec(
            num_scalar_prefetch=1,
            grid=(bsz,),
            in_specs=[
                pl.BlockSpec((seq, ATTN_WIDTH), lambda b, sk: (b, COL_Q // ATTN_WIDTH)),
                pl.BlockSpec((seq, KV_WIDTH), lambda b, sk: (b, COL_K // KV_WIDTH)),
                pl.BlockSpec((seq, KV_WIDTH), lambda b, sk: (b, COL_V // KV_WIDTH)),
                pl.BlockSpec((1, seq, LANES), lambda b, sk: (b, 0, 0)),
                pl.BlockSpec((1, seq, LANES), lambda b, sk: (b, 0, 0)),
                pl.BlockSpec((1, LANES), lambda b, sk: (0, 0)),
                pl.BlockSpec((1, LANES), lambda b, sk: (0, 0)),
            ],
            out_specs=pl.BlockSpec((seq, ATTN_WIDTH), lambda b, sk: (b, 0)),
            scratch_shapes=[
                pltpu.VMEM((2 * N_KV_HEADS, seq + 2 * ATTN_BLOCK, LANES), BF16),
                pltpu.VMEM((2 * N_KV_HEADS, seq + 2 * ATTN_BLOCK, LANES), BF16),
            ],
        ),
        out_shape=jax.ShapeDtypeStruct((t, ATTN_WIDTH), BF16),
        compiler_params=_params("arbitrary"),
        name="window_attn",
    )(sink, proj, proj, proj, cos_t, sin_t, q_gain, k_gain)


def _ssd_kernel(dch_ref, xs_ref, b_ref, c_ref, z_ref, cwx_ref, cwb_ref, cwc_ref, cbx_ref, cbb_ref, cbc_ref,
                lmat_ref, rmat_ref, xmat_ref, ex_ref, dskip_ref, ng_ref, o_ref,
                upad, xs_c, b_c, c_c, y_acc, st, d_scr, m_scr):
    q = SSM_CHUNK
    s = xs_ref.shape[0]
    nc = s // q
    pad = 16
    halo = SSM_CONV // 2
    win = q + pad
    width = GROUP_WIDTH + 2 * SSM_STATE

    upad[0:pad, :] = jnp.zeros((pad, width), F32)
    upad[pad + s:2 * pad + s, :] = jnp.zeros((pad, width), F32)

    def fill(c, carry):
        r0 = pl.multiple_of(c * q, q)
        rows = pl.ds(r0, q)
        dst = pl.ds(r0 + pad, q)
        upad[dst, 0:GROUP_WIDTH] = xs_ref[rows, :].astype(F32)
        upad[dst, GROUP_WIDTH:GROUP_WIDTH + SSM_STATE] = b_ref[rows, :].astype(F32)
        upad[dst, GROUP_WIDTH + SSM_STATE:width] = c_ref[rows, :].astype(F32)
        return carry

    lax.fori_loop(0, nc, fill, 0)

    def conv_cols(r0, lo_col, n_col, w_ref, bias_ref, dst_ref):
        window = upad[pl.ds(r0 + pad - SUBLANES, win), lo_col:lo_col + n_col]
        acc = jnp.zeros((q, n_col), F32) + bias_ref[...]
        for k in range(SSM_CONV):
            shift = SUBLANES - halo + k
            acc = acc + pltpu.roll(window, win - shift, 0)[0:q] * w_ref[k:k + 1, :]
        dst_ref[pl.ds(r0, q), :] = acc * (1.0 / (1.0 + jnp.exp(-acc)))

    def conv(c, carry):
        r0 = pl.multiple_of(c * q, q)
        for j in range(GROUP_WIDTH // LANES):
            conv_cols(r0, j * LANES, LANES, cwx_ref.at[:, j * LANES:(j + 1) * LANES],
                      cbx_ref.at[:, j * LANES:(j + 1) * LANES], xs_c.at[:, j * LANES:(j + 1) * LANES])
        conv_cols(r0, GROUP_WIDTH, SSM_STATE, cwb_ref, cbb_ref, b_c)
        conv_cols(r0, GROUP_WIDTH + SSM_STATE, SSM_STATE, cwc_ref, cbc_ref, c_c)
        return carry

    lax.fori_loop(0, nc, conv, 0)

    row = lax.broadcasted_iota(jnp.int32, (q, q), 0)
    col = lax.broadcasted_iota(jnp.int32, (q, q), 1)
    lower = row >= col
    on_diag = row == col
    head_of_row = row // LR_STRIDE
    low_half = lax.broadcasted_iota(jnp.int32, (q, LANES), 1) < SSM_HEAD_DIM
    gw = GROUP_WIDTH

    def expand(xm, block):
        return jnp.dot(xm, ex_ref[:, block * gw:(block + 1) * gw], preferred_element_type=F32)

    def decay_rows(c, first):
        return jnp.concatenate(
            [jnp.full((1, SSM_HEAD_DIM), dch_ref[0, 0, c, first + h], F32) for h in range(SSM_HPG)], axis=-1)

    st[...] = jnp.zeros_like(st)

    def fwd(c, carry):
        r0 = pl.multiple_of(c * q, q)
        rows = pl.ds(r0, q)
        xs = xs_c[rows, :]
        bmb = b_c[rows, :].astype(BF16)
        cm = c_c[rows, :].astype(BF16)
        lm = lmat_ref[rows, :]
        rm = rmat_ref[:, rows]
        xm = xmat_ref[rows, :]
        cb = lax.dot_general(cm, bmb, (((1,), (1,)), ((), ())), preferred_element_type=F32)
        xs_b = xs.astype(BF16)
        zero_b = jnp.zeros((q, LANES), BF16)
        for h in range(SSM_HPG):
            rhs = jnp.concatenate([jnp.where(head_of_row == h, rm, zero_b),
                                   jnp.where(head_of_row == SSM_HPG + h, rm, zero_b)], axis=1)
            d_scr[:, 2 * h * q:2 * (h + 1) * q] = jnp.dot(lm, rhs, preferred_element_type=F32)
        for h in range(SSM_HPG):
            d_f = d_scr[:, 2 * h * q:(2 * h + 1) * q]
            d_b = d_scr[:, (2 * h + 1) * q:(2 * h + 2) * q]
            dt_row = (SSM_HPG + h) * LR_STRIDE + DT_ROWS
            dt_b = rm[dt_row:dt_row + 1, :].astype(F32) + rm[dt_row + 1:dt_row + 2, :].astype(F32)
            e = jnp.exp(jnp.where(lower, d_f, d_b)) + jnp.where(on_diag, dt_b, 0.0)
            m_scr[:, h * q:(h + 1) * q] = (cb * e).astype(BF16)
        ys = []
        for pair in range(SSM_HPG // 2):
            xpair = xs_b[:, pair * LANES:(pair + 1) * LANES]
            stacked = jnp.concatenate([jnp.where(low_half, xpair, zero_b), jnp.where(low_half, zero_b, xpair)], axis=0)
            ys.append(jnp.dot(m_scr[:, 2 * pair * q:2 * (pair + 1) * q], stacked, preferred_element_type=F32))
        y = jnp.concatenate(ys, axis=-1) + xs * dskip_ref[...]
        state = st[...]
        y_off = jnp.dot(cm, state.astype(BF16), preferred_element_type=F32)
        y_acc[rows, :] = y + y_off * expand(xm, 0)
        xw = (xs * expand(xm, 1)).astype(BF16)
        upd = lax.dot_general(bmb, xw, (((0,), (0,)), ((), ())), preferred_element_type=F32)
        st[...] = state * decay_rows(c, 0) + upd
        return carry

    lax.fori_loop(0, nc, fwd, 0)

    st[...] = jnp.zeros_like(st)

    def bwd(i, carry):
        c = nc - 1 - i
        r0 = pl.multiple_of(c * q, q)
        rows = pl.ds(r0, q)
        xs = xs_c[rows, :]
        bmb = b_c[rows, :].astype(BF16)
        cm = c_c[rows, :].astype(BF16)
        xm = xmat_ref[rows, :]
        state = st[...]
        y_off = jnp.dot(cm, state.astype(BF16), preferred_element_type=F32)
        y = y_acc[rows, :] + y_off * expand(xm, 2)
        xw = (xs * expand(xm, 3)).astype(BF16)
        upd = lax.dot_general(bmb, xw, (((0,), (0,)), ((), ())), preferred_element_type=F32)
        st[...] = state * decay_rows(c, SSM_HPG) + upd
        zz = z_ref[rows, :].astype(F32)
        y = y * (zz * (1.0 / (1.0 + jnp.exp(-zz))))
        ms = jnp.mean(y * y, axis=-1, keepdims=True)
        o_ref[rows, :] = (y * lax.rsqrt(ms + NORM_EPS) * ng_ref[...]).astype(BF16)
        return carry

    lax.fori_loop(0, nc, bwd, 0)


def _ssd_call(proj, conv_w, conv_b, lmat, rmat, xmat, dch, d_skip, norm_g, bsz, seq):
    t = proj.shape[0]
    g_n = SSM_GROUPS
    q = SSM_CHUNK
    gw = GROUP_WIDTH
    width = GROUP_WIDTH + 2 * SSM_STATE
    ex = _expansion_constant()
    return pl.pallas_call(
        _ssd_kernel,
        grid=(bsz, g_n),
        in_specs=[
            pl.BlockSpec((1, 1, seq // q, DT_COLS), lambda b, g: (b, g, 0, 0), memory_space=pltpu.SMEM),
            pl.BlockSpec((seq, gw), lambda b, g: (b, COL_XS // gw + g)),
            pl.BlockSpec((seq, SSM_STATE), lambda b, g: (b, COL_B // SSM_STATE + g)),
            pl.BlockSpec((seq, SSM_STATE), lambda b, g: (b, COL_C // SSM_STATE + g)),
            pl.BlockSpec((seq, gw), lambda b, g: (b, COL_Z // gw + g)),
            pl.BlockSpec((SSM_CONV, gw), lambda b, g: (0, g)),
            pl.BlockSpec((SSM_CONV, SSM_STATE), lambda b, g: (0, SSM_INNER // SSM_STATE + g)),
            pl.BlockSpec((SSM_CONV, SSM_STATE), lambda b, g: (0, SSM_INNER // SSM_STATE + g_n + g)),
            pl.BlockSpec((1, gw), lambda b, g: (0, g)),
            pl.BlockSpec((1, SSM_STATE), lambda b, g: (0, SSM_INNER // SSM_STATE + g)),
            pl.BlockSpec((1, SSM_STATE), lambda b, g: (0, SSM_INNER // SSM_STATE + g_n + g)),
            pl.BlockSpec((seq, LANES), lambda b, g: (b, g)),
            pl.BlockSpec((LANES, seq), lambda b, g: (b * g_n + g, 0)),
            pl.BlockSpec((seq, LANES), lambda b, g: (b, g)),
            pl.BlockSpec(ex.shape, lambda b, g: (0, 0)),
            pl.BlockSpec((1, gw), lambda b, g: (0, g)),
            pl.BlockSpec((1, gw), lambda b, g: (0, g)),
        ],
        out_specs=pl.BlockSpec((seq, gw), lambda b, g: (b, g)),
        out_shape=jax.ShapeDtypeStruct((t, SSM_INNER), BF16),
        scratch_shapes=[
            pltpu.VMEM((seq + 32, width), F32),
            pltpu.VMEM((seq, gw), F32),
            pltpu.VMEM((seq, SSM_STATE), F32),
            pltpu.VMEM((seq, SSM_STATE), F32),
            pltpu.VMEM((seq, gw), F32),
            pltpu.VMEM((SSM_STATE, gw), F32),
            pltpu.VMEM((q, 2 * SSM_HPG * q), F32),
            pltpu.VMEM((q, SSM_HPG * q), BF16),
        ],
        compiler_params=_params("arbitrary", "arbitrary"),
        name="ssd_mixer",
    )(dch, proj, proj, proj, proj, conv_w, conv_w, conv_w, conv_b, conv_b, conv_b,
      lmat, rmat, xmat, ex, d_skip, norm_g)


def _outproj_kernel(attn_ref, ssm_ref, ga_ref, gs_ref, x_ref, mod_ref, wa_ref, ws_ref, wo_ref, o_ref):
    ya = jnp.dot(attn_ref[...], wa_ref[...], preferred_element_type=F32)
    ys = jnp.dot(ssm_ref[...], ws_ref[...], preferred_element_type=F32)
    ga = ga_ref[...].astype(F32)
    gs = gs_ref[...].astype(F32)
    merged = ya * (1.0 / (1.0 + jnp.exp(-ga))) + ys * (1.0 / (1.0 + jnp.exp(-gs)))
    y = jnp.dot(merged.astype(BF16), wo_ref[...], preferred_element_type=F32)
    o_ref[...] = x_ref[...] + mod_ref[0, 2:3, :] * y


def _outproj_call(attn, ssm, proj, x2, mod, w_attn_o, w_ssm_o, w_out, seq):
    t, d = x2.shape
    tm = 512
    per_batch = seq // tm
    const = lambda i: (0, 0)
    return pl.pallas_call(
        _outproj_kernel,
        grid=(t // tm,),
        in_specs=[
            pl.BlockSpec((tm, ATTN_WIDTH), lambda i: (i, 0)),
            pl.BlockSpec((tm, SSM_INNER), lambda i: (i, 0)),
            pl.BlockSpec((tm, d), lambda i: (i, COL_GATE_ATTN // d)),
            pl.BlockSpec((tm, d), lambda i: (i, COL_GATE_SSM // d)),
            pl.BlockSpec((tm, d), lambda i: (i, 0)),
            pl.BlockSpec((1, N_MOD, d), lambda i: (i // per_batch, 0, 0)),
            pl.BlockSpec((ATTN_WIDTH, d), const),
            pl.BlockSpec((SSM_INNER, d), const),
            pl.BlockSpec((d, d), const),
        ],
        out_specs=pl.BlockSpec((tm, d), lambda i: (i, 0)),
        out_shape=jax.ShapeDtypeStruct((t, d), F32),
        compiler_params=_params("arbitrary"),
        name="out_proj",
    )(attn, ssm, proj, proj, x2, mod, w_attn_o, w_ssm_o, w_out)


def _permute_in_proj(w_in):
    w_main = jnp.concatenate(
        [w_in[:, _O_Q:_O_K], w_in[:, _O_GATES:], w_in[:, _O_Z:_O_XBC], w_in[:, _O_XBC:_O_XBC + SSM_INNER],
         w_in[:, _O_K:_O_Z], w_in[:, _O_XBC + SSM_INNER:_O_DT]], axis=1).astype(BF16)
    return w_main


def _dt_permutation():
    idx = np.zeros(2 * SSM_HEADS, np.int32)
    for g in range(SSM_GROUPS):
        for d in range(2):
            for r in range(SSM_HPG):
                idx[g * DT_COLS + d * SSM_HPG + r] = d * SSM_HEADS + g * SSM_HPG + r
    return idx


def _pad_lanes(a):
    return jnp.pad(a, [(0, 0)] * (a.ndim - 1) + [(0, LANES - a.shape[-1])])


def _mixer_layer(x2, mod, cos_t, sin_t, bsz, seq, norm_g, w_in, q_norm_g, k_norm_g, attn_sink, conv_w, conv_b,
                 a_log, dt_bias, ssm_d, ssm_norm_g, w_attn_o, w_ssm_o, w_out):
    d = x2.shape[1]
    gain = norm_g.reshape(1, d)
    perm = _dt_permutation()
    w_dt = _pad_lanes(w_in[:, _O_DT:_O_GATES][:, perm])
    dt_b = _pad_lanes(dt_bias.reshape(-1)[perm].reshape(1, -1))
    a_lg = _pad_lanes(a_log.reshape(-1)[perm].reshape(1, -1))

    proj = _inproj_call(x2, mod, gain, _permute_in_proj(w_in), seq)
    lmat, rmat, xmat, dch = _dt_call(x2, mod, gain, w_dt, dt_b, a_lg, bsz, seq)
    nc = seq // SSM_CHUNK
    dch = dch[:, :, 0, :2 * SSM_HEADS].reshape(bsz, nc, SSM_GROUPS, DT_COLS).transpose(0, 2, 1, 3)

    attn = _attn_call(proj, cos_t, sin_t, jnp.tile(q_norm_g, 2).reshape(1, LANES),
                      jnp.tile(k_norm_g, 2).reshape(1, LANES), attn_sink, bsz, seq)
    ssm = _ssd_call(proj, conv_w, conv_b.reshape(1, -1), lmat, rmat, xmat, dch,
                    jnp.repeat(ssm_d, SSM_HEAD_DIM).reshape(1, SSM_INNER), ssm_norm_g.reshape(1, SSM_INNER),
                    bsz, seq)
    return _outproj_call(attn, ssm, proj, x2, mod, w_attn_o.astype(BF16), w_ssm_o.astype(BF16),
                         w_out.astype(BF16), seq)


ROUTE_BLOCK = 256
SLAB = SUBLANES
LOCAL_ROWS = ROUTE_BLOCK * TOP_K + ROUTE_BLOCK
LOCAL_SLABS = LOCAL_ROWS // SLAB
TILE_ROWS = 256
TILE_SLABS = TILE_ROWS // SLAB


def _route_kernel(x_ref, mod_ref, g_ref, rw_ref, rb_ref, xloc_ref, meta_ref, cnt_ref):
    tb = ROUTE_BLOCK
    h = _prenorm_modulate(x_ref[...], g_ref[...], mod_ref[0, 3:4, :], mod_ref[0, 4:5, :])
    logits = jnp.dot(h, rw_ref[...], precision=HIGHEST, preferred_element_type=F32) + rb_ref[...]
    v = logits.T[0:N_EXPERTS, :]
    erow = lax.broadcasted_iota(jnp.int32, (N_EXPERTS, tb), 0)
    hots, tops = [], []
    for _ in range(TOP_K):
        mk = jnp.max(v, axis=0, keepdims=True)
        first = jnp.min(jnp.where(v == mk, erow, N_EXPERTS), axis=0, keepdims=True)
        hot = erow == first
        v = jnp.where(hot, -jnp.inf, v)
        hots.append(hot)
        tops.append(mk)
    ps = [jnp.exp(mk - tops[0]) for mk in tops]
    denom = ps[0] + ps[1] + ps[2] + ps[3]
    sel = jnp.zeros((N_EXPERTS, tb), F32)
    for hot in hots:
        sel = sel + jnp.where(hot, 1.0, 0.0)
    ti = lax.broadcasted_iota(jnp.int32, (tb, tb), 0)
    tj = lax.broadcasted_iota(jnp.int32, (tb, tb), 1)
    before = jnp.where(ti < tj, 1.0, 0.0).astype(BF16)
    rank = jnp.dot(sel.astype(BF16), before, preferred_element_type=F32)
    count = jnp.sum(sel, axis=1, keepdims=True)
    slabs = jnp.floor((count + (SLAB - 1)) * (1.0 / SLAB))
    ei = lax.broadcasted_iota(jnp.int32, (N_EXPERTS, N_EXPERTS), 0)
    ej = lax.broadcasted_iota(jnp.int32, (N_EXPERTS, N_EXPERTS), 1)
    earlier = jnp.where(ej < ei, 1.0, 0.0).astype(BF16)
    slabs_b = jnp.broadcast_to(slabs, (N_EXPERTS, tb))
    start = jnp.dot(earlier, slabs_b.astype(BF16), preferred_element_type=F32)
    dest = start * SLAB + rank
    dks = [jnp.sum(jnp.where(hot, dest, 0.0), axis=0, keepdims=True) for hot in hots]
    meta_ref[0] = jnp.concatenate(dks + [p / denom for p in ps], axis=0)
    cnt_ref[0] = slabs_b[:, 0:LANES]
    hb = h.astype(BF16)
    dki = [dk.astype(jnp.int32) for dk in dks]
    for c in range(LOCAL_ROWS // tb):
        ri = lax.broadcasted_iota(jnp.int32, (tb, tb), 0) + c * tb
        onehot = jnp.where((ri == dki[0]) | (ri == dki[1]) | (ri == dki[2]) | (ri == dki[3]), 1.0, 0.0)
        xloc_ref[0, c * tb:(c + 1) * tb, :] = jnp.dot(onehot.astype(BF16), hb, preferred_element_type=F32)


def _route_call(x2, mod, gain, router_w, router_b, seq):
    t, d = x2.shape
    tb = ROUTE_BLOCK
    nblk = t // tb
    per_batch = seq // tb
    return pl.pallas_call(
        _route_kernel,
        grid=(nblk,),
        in_specs=[
            pl.BlockSpec((tb, d), lambda i: (i, 0)),
            pl.BlockSpec((1, N_MOD, d), lambda i: (i // per_batch, 0, 0)),
            pl.BlockSpec((1, d), lambda i: (0, 0)),
            pl.BlockSpec((d, LANES), lambda i: (0, 0)),
            pl.BlockSpec((1, LANES), lambda i: (0, 0)),
        ],
        out_specs=[
            pl.BlockSpec((1, LOCAL_ROWS, d), lambda i: (i, 0, 0)),
            pl.BlockSpec((1, 2 * TOP_K, tb), lambda i: (i, 0, 0)),
            pl.BlockSpec((1, N_EXPERTS, LANES), lambda i: (i, 0, 0)),
        ],
        out_shape=[
            jax.ShapeDtypeStruct((nblk, LOCAL_ROWS, d), F32),
            jax.ShapeDtypeStruct((nblk, 2 * TOP_K, tb), F32),
            jax.ShapeDtypeStruct((nblk, N_EXPERTS, LANES), F32),
        ],
        compiler_params=_params("arbitrary"),
        name="moe_route",
    )(x2, mod, gain, _pad_lanes(router_w), _pad_lanes(router_b.reshape(1, -1)))


def _slab_plan(slab_counts):
    nblk = slab_counts.shape[0]
    max_slabs = nblk * LOCAL_SLABS + N_EXPERTS * (TILE_SLABS - 1)
    max_tiles = -(-max_slabs // TILE_SLABS)
    c8 = slab_counts.astype(jnp.int32)
    local_start = jnp.cumsum(c8, axis=1) - c8
    per_expert = jnp.sum(c8, axis=0)
    tiles_e = (per_expert + TILE_SLABS - 1) // TILE_SLABS
    tile_start = jnp.cumsum(tiles_e) - tiles_e
    num_tiles = jnp.sum(tiles_e)
    expert_start = tile_start * TILE_SLABS
    block_off = jnp.cumsum(c8, axis=0) - c8
    seg_start = expert_start[None, :] + block_off

    sl = jnp.arange(LOCAL_SLABS, dtype=jnp.int32)[None, :, None]
    in_seg = (local_start[:, None, :] <= sl) & (sl < (local_start + c8)[:, None, :])
    slab_pos = jnp.sum(jnp.where(in_seg, (seg_start - local_start)[:, None, :] + sl, 0), axis=-1).reshape(-1)

    p = jnp.arange(max_tiles * TILE_SLABS, dtype=jnp.int32)[:, None]
    in_exp = (expert_start[None, :] <= p) & (p < (expert_start + per_expert)[None, :])
    off = p - jnp.sum(jnp.where(in_exp, expert_start[None, :], 0), axis=-1, keepdims=True)
    table = jnp.concatenate([block_off.T, c8.T, local_start.T], axis=1).astype(F32)
    picked = jnp.round(jnp.dot(in_exp.astype(F32), table, precision=HIGHEST)).astype(jnp.int32)
    boff, cnt, lst = picked[:, :nblk], picked[:, nblk:2 * nblk], picked[:, 2 * nblk:]
    in_blk = (boff <= off) & (off < boff + cnt)
    blk_base = jnp.arange(nblk, dtype=jnp.int32)[None, :] * LOCAL_SLABS
    slab_src = jnp.sum(jnp.where(in_blk, blk_base + lst - boff + off, 0), axis=-1)

    ti = jnp.arange(max_tiles, dtype=jnp.int32)
    tile_expert = jnp.clip(jnp.sum(tile_start[None, :] <= jnp.minimum(ti, num_tiles - 1)[:, None], axis=-1) - 1,
                           0, N_EXPERTS - 1)
    return tile_expert.astype(jnp.int32), slab_src.astype(jnp.int32), slab_pos.astype(jnp.int32), \
        num_tiles.reshape(1).astype(jnp.int32), max_tiles


def _expert_kernel(te_ref, src_ref, nt_ref, xloc_hbm, wg_ref, bg_ref, wu_ref, bu_ref, wd_ref, bd_ref, y_ref,
                   xbuf, sem, wg_s, wu_s, wd_s):
    i = pl.program_id(0)
    nt = nt_ref[0]

    def slab_copy(tile, slot, j):
        return pltpu.make_async_copy(xloc_hbm.at[src_ref[tile * TILE_SLABS + j]], xbuf.at[slot, j], sem.at[slot])

    def issue(tile, slot):
        for j in range(TILE_SLABS):
            slab_copy(tile, slot, j).start()

    @pl.when(i == 0)
    def _():
        issue(0, 0)

    @pl.when(i + 1 < nt)
    def _():
        issue(i + 1, (i + 1) % 2)

    @pl.when(i < nt)
    def _():
        slot = i % 2
        for j in range(TILE_SLABS):
            slab_copy(i, slot, j).wait()
        new_expert = jnp.logical_or(i == 0, te_ref[i] != te_ref[jnp.maximum(i - 1, 0)])

        @pl.when(new_expert)
        def _():
            rows = 128
            for src, dst in ((wg_ref, wg_s), (wu_ref, wu_s), (wd_ref, wd_s)):
                def cast(c, carry, src=src, dst=dst):
                    r = pl.ds(pl.multiple_of(c * rows, rows), rows)
                    dst[r, :] = src[0, 0, r, :].astype(BF16)
                    return carry
                lax.fori_loop(0, src.shape[2] // rows, cast, 0)

        x = xbuf[slot].reshape(TILE_ROWS, xbuf.shape[-1]).astype(BF16)
        gate = jnp.dot(x, wg_s[...], preferred_element_type=F32) + bg_ref[0, 0]
        up = jnp.dot(x, wu_s[...], preferred_element_type=F32) + bu_ref[0, 0]
        glu = jnp.minimum(gate, SWIGLU_LIMIT)
        lin = jnp.clip(up, -SWIGLU_LIMIT, SWIGLU_LIMIT)
        act = glu * (1.0 / (1.0 + jnp.exp(-SWIGLU_ALPHA * glu))) * (lin + 1.0)
        y_ref[...] = jnp.dot(act.astype(BF16), wd_s[...], preferred_element_type=F32) + bd_ref[0, 0]

    @pl.when(i >= nt)
    def _():
        y_ref[...] = jnp.zeros_like(y_ref)


def _expert_call(tile_expert, slab_src, num_tiles, max_tiles, xloc, layer, w_gate, b_gate, w_up, b_up, w_down, b_down):
    nblk, _, d = xloc.shape
    depth, _, _, ff = w_gate.shape
    xloc3 = xloc.reshape(nblk * LOCAL_SLABS, SLAB, d)
    wspec = lambda k, n: pl.BlockSpec((1, 1, k, n), lambda i, te, ss, nt: (layer, te[i], 0, 0))
    return pl.pallas_call(
        _expert_kernel,
        grid_spec=pltpu.PrefetchScalarGridSpec(
            num_scalar_prefetch=3,
            grid=(max_tiles,),
            in_specs=[
                pl.BlockSpec(memory_space=pl.ANY),
                wspec(d, ff), wspec(1, ff), wspec(d, ff), wspec(1, ff), wspec(ff, d), wspec(1, d),
            ],
            out_specs=pl.BlockSpec((TILE_ROWS, d), lambda i, te, ss, nt: (i, 0)),
            scratch_shapes=[
                pltpu.VMEM((2, TILE_SLABS, SLAB, d), F32),
                pltpu.SemaphoreType.DMA((2,)),
                pltpu.VMEM((d, ff), BF16),
                pltpu.VMEM((d, ff), BF16),
                pltpu.VMEM((ff, d), BF16),
            ],
        ),
        out_shape=jax.ShapeDtypeStruct((max_tiles * TILE_ROWS, d), F32),
        compiler_params=_params("arbitrary"),
        name="moe_experts",
    )(tile_expert, slab_src, num_tiles, xloc3, w_gate, b_gate.reshape(depth, N_EXPERTS, 1, ff), w_up,
      b_up.reshape(depth, N_EXPERTS, 1, ff), w_down, b_down.reshape(depth, N_EXPERTS, 1, d))


def _combine_kernel(pos_ref, y_hbm, meta_ref, x_ref, mod_ref, o_ref, ybuf, sem):
    b = pl.program_id(0)
    nb = pl.num_programs(0)
    tb = ROUTE_BLOCK

    def slab_copy(blk, slot, j):
        return pltpu.make_async_copy(y_hbm.at[pos_ref[blk * LOCAL_SLABS + j]], ybuf.at[slot, j], sem.at[slot])

    def issue(blk, slot):
        def body(j, carry):
            slab_copy(blk, slot, j).start()
            return carry
        lax.fori_loop(0, LOCAL_SLABS, body, 0)

    @pl.when(b == 0)
    def _():
        issue(0, 0)

    @pl.when(b + 1 < nb)
    def _():
        issue(b + 1, (b + 1) % 2)

    slot = b % 2

    def wait_body(j, carry):
        slab_copy(b, slot, j).wait()
        return carry
    lax.fori_loop(0, LOCAL_SLABS, wait_body, 0)

    meta = meta_ref[0]
    dki = [meta[k:k + 1, :].astype(jnp.int32) for k in range(TOP_K)]
    wks = [meta[TOP_K + k:TOP_K + k + 1, :] for k in range(TOP_K)]
    acc = jnp.zeros((tb, x_ref.shape[-1]), F32)
    slabs_per_chunk = tb // SLAB
    for c in range(LOCAL_ROWS // tb):
        ri = lax.broadcasted_iota(jnp.int32, (tb, tb), 0) + c * tb
        hits = [ri == dk for dk in dki]
        onehot = jnp.where(hits[0] | hits[1] | hits[2] | hits[3], 1.0, 0.0)
        weighted = jnp.where(hits[0], wks[0], jnp.where(hits[1], wks[1], jnp.where(hits[2], wks[2],
                             jnp.where(hits[3], wks[3], 0.0))))
        w_row = jnp.sum(weighted, axis=1, keepdims=True)
        y = ybuf[slot, c * slabs_per_chunk:(c + 1) * slabs_per_chunk].reshape(tb, x_ref.shape[-1])
        yw = (y * w_row).astype(BF16)
        acc = acc + lax.dot_general(onehot.astype(BF16), yw, (((0,), (0,)), ((), ())), preferred_element_type=F32)
    o_ref[...] = x_ref[...] + mod_ref[0, 5:6, :] * acc


def _combine_call(slab_pos, y, meta, x2, mod, seq):
    t, d = x2.shape
    tb = ROUTE_BLOCK
    nblk = t // tb
    per_batch = seq // tb
    y3 = y.reshape(y.shape[0] // SLAB, SLAB, d)
    return pl.pallas_call(
        _combine_kernel,
        grid_spec=pltpu.PrefetchScalarGridSpec(
            num_scalar_prefetch=1,
            grid=(nblk,),
            in_specs=[
                pl.BlockSpec(memory_space=pl.ANY),
                pl.BlockSpec((1, 2 * TOP_K, tb), lambda i, sp: (i, 0, 0)),
                pl.BlockSpec((tb, d), lambda i, sp: (i, 0)),
                pl.BlockSpec((1, N_MOD, d), lambda i, sp: (i // per_batch, 0, 0)),
            ],
            out_specs=pl.BlockSpec((tb, d), lambda i, sp: (i, 0)),
            scratch_shapes=[
                pltpu.VMEM((2, LOCAL_SLABS, SLAB, d), F32),
                pltpu.SemaphoreType.DMA((2,)),
            ],
        ),
        out_shape=jax.ShapeDtypeStruct((t, d), F32),
        compiler_params=_params("arbitrary"),
        name="moe_combine",
    )(slab_pos, y3, meta, x2, mod)


def _moe_layer(x2, mod, seq, layer, norm_g, router_w, router_b, w_gate, b_gate, w_up, b_up, w_down, b_down):
    d = x2.shape[1]
    xloc, meta, cnt = _route_call(x2, mod, norm_g.reshape(1, d), router_w, router_b, seq)
    tile_expert, slab_src, slab_pos, num_tiles, max_tiles = _slab_plan(cnt[:, :, 0])
    y = _expert_call(tile_expert, slab_src, num_tiles, max_tiles, xloc, layer, w_gate, b_gate, w_up, b_up, w_down,
                     b_down)
    return _combine_call(slab_pos, y, meta, x2, mod, seq)


def kernel(x, c, positions, ada_w, ada_b, norm1_g, norm2_g, w_in, q_norm_g, k_norm_g, attn_sink, conv_w, conv_b,
           a_log, dt_bias, ssm_d, ssm_norm_g, w_attn_o, w_ssm_o, w_out, router_w, router_b, exp_w_gate, exp_b_gate,
           exp_w_up, exp_b_up, exp_w_down, exp_b_down):
    bsz, seq, d = x.shape
    depth = ada_w.shape[0]
    assert d == D_MODEL and seq % 512 == 0 and (bsz * seq) % ROUTE_BLOCK == 0
    mod_all = _ada_call(c, ada_w, ada_b).reshape(depth, bsz, N_MOD, d)
    cos_t, sin_t = _rope_call(positions)
    x2 = x.reshape(bsz * seq, d)
    for l in range(depth):
        mod = mod_all[l]
        x2 = _mixer_layer(x2, mod, cos_t, sin_t, bsz, seq, norm1_g[l], w_in[l], q_norm_g[l], k_norm_g[l],
                          attn_sink[l], conv_w[l], conv_b[l], a_log[l], dt_bias[l], ssm_d[l], ssm_norm_g[l],
                          w_attn_o[l], w_ssm_o[l], w_out[l])
        x2 = _moe_layer(x2, mod, seq, l, norm2_g[l], router_w[l], router_b[l], exp_w_gate, exp_b_gate,
                        exp_w_up, exp_b_up, exp_w_down, exp_b_down)
    return x2.reshape(bsz, seq, d)
```

```python
import functools

import jax
import jax.numpy as jnp
import numpy as np
from jax import lax
from jax.experimental import pallas as pl
from jax.experimental.pallas import tpu as pltpu

F32 = jnp.float32
BF16 = jnp.bfloat16
HIGHEST = lax.Precision.HIGHEST

LANES = 128
SUBLANES = 8
VMEM_LIMIT = 56 * 1024 * 1024

D_MODEL = 1024
NORM_EPS = 1e-5
N_MOD = 6

HEAD_DIM = 64
N_Q_HEADS = 16
N_KV_HEADS = 4
ATTN_WIDTH = N_Q_HEADS * HEAD_DIM
KV_WIDTH = N_KV_HEADS * HEAD_DIM
WINDOW = 128
ATTN_BLOCK = 128
ROPE_THETA = 500000.0
ROPE_DIMS = HEAD_DIM // 4
ROPE_HALF = ROPE_DIMS // 2
NEG_BIG = -1e30

SSM_INNER = 2 * D_MODEL
SSM_HEAD_DIM = 64
SSM_HEADS = SSM_INNER // SSM_HEAD_DIM
SSM_GROUPS = 4
SSM_HPG = SSM_HEADS // SSM_GROUPS
SSM_STATE = 128
SSM_CONV = 5
SSM_CHUNK = 128
GROUP_WIDTH = SSM_INNER // SSM_GROUPS
DT_COLS = 2 * SSM_HPG

N_EXPERTS = 32
TOP_K = 4
SWIGLU_LIMIT = 7.0
SWIGLU_ALPHA = 1.702

COL_Q = 0
COL_GATE_ATTN = 1024
COL_GATE_SSM = 2048
COL_Z = 3072
COL_XS = 5120
COL_K = 7168
COL_V = 7424
COL_B = 7680
COL_C = 8192
PROJ_WIDTH = 8704
PROJ_N_TILE = PROJ_WIDTH // 4

_O_Q, _O_K, _O_V, _O_Z, _O_XBC, _O_DT, _O_GATES = 0, 1024, 1280, 1536, 3584, 6656, 6720


def _params(*sem):
    return pltpu.CompilerParams(dimension_semantics=sem, vmem_limit_bytes=VMEM_LIMIT)


def _prenorm_modulate(x, gain, shift, scale):
    ms = jnp.mean(x * x, axis=-1, keepdims=True)
    return (x * lax.rsqrt(ms + NORM_EPS) * gain) * (1.0 + scale) + shift


def _ada_kernel(c_ref, w_ref, b_ref, o_ref):
    c = c_ref[...]
    c_act = c * (1.0 / (1.0 + jnp.exp(-c)))
    o_ref[0] = jnp.dot(c_act, w_ref[0], precision=HIGHEST, preferred_element_type=F32) + b_ref[0]


def _ada_call(c, ada_w, ada_b):
    depth, d, n = ada_w.shape
    bsz = c.shape[0]
    tn = 1536
    return pl.pallas_call(
        _ada_kernel,
        grid=(depth, n // tn),
        in_specs=[
            pl.BlockSpec((bsz, d), lambda l, j: (0, 0)),
            pl.BlockSpec((1, d, tn), lambda l, j: (l, 0, j)),
            pl.BlockSpec((1, 1, tn), lambda l, j: (l, 0, j)),
        ],
        out_specs=pl.BlockSpec((1, bsz, tn), lambda l, j: (l, 0, j)),
        out_shape=jax.ShapeDtypeStruct((depth, bsz, n), F32),
        compiler_params=_params("arbitrary", "arbitrary"),
        name="ada_mod",
    )(c, ada_w, ada_b.reshape(depth, 1, n))


def _rope_kernel(pos_ref, freq_ref, sign_ref, cos_ref, sin_ref):
    ang = pos_ref[0].astype(F32) * freq_ref[...]
    sign = sign_ref[...]
    cos_ref[0] = jnp.where(sign == 0.0, 1.0, jnp.cos(ang))
    sin_ref[0] = jnp.sin(ang) * sign


def _rope_call(positions):
    bsz, s = positions.shape
    lane = np.arange(LANES) % HEAD_DIM
    inv_freq = ROPE_THETA ** (-np.arange(0, ROPE_DIMS, 2, dtype=np.float32) / ROPE_DIMS)
    freq = np.where(lane < ROPE_DIMS, inv_freq[lane % ROPE_HALF], 0.0).astype(np.float32)
    sign = np.where(lane < ROPE_HALF, -1.0, np.where(lane < ROPE_DIMS, 1.0, 0.0)).astype(np.float32)
    ts = 512
    spec = pl.BlockSpec((1, ts, LANES), lambda b, i: (b, i, 0))
    return pl.pallas_call(
        _rope_kernel,
        grid=(bsz, s // ts),
        in_specs=[
            pl.BlockSpec((1, ts, 1), lambda b, i: (b, i, 0)),
            pl.BlockSpec((1, LANES), lambda b, i: (0, 0)),
            pl.BlockSpec((1, LANES), lambda b, i: (0, 0)),
        ],
        out_specs=[spec, spec],
        out_shape=[jax.ShapeDtypeStruct((bsz, s, LANES), F32)] * 2,
        compiler_params=_params("arbitrary", "arbitrary"),
        name="rope_tables",
    )(positions.reshape(bsz, s, 1), jnp.asarray(freq).reshape(1, LANES), jnp.asarray(sign).reshape(1, LANES))


def _inproj_kernel(x_ref, mod_ref, g_ref, w_ref, o_ref):
    h = _prenorm_modulate(x_ref[...], g_ref[...], mod_ref[0, 0:1, :], mod_ref[0, 1:2, :])
    o_ref[...] = jnp.dot(h.astype(BF16), w_ref[...], preferred_element_type=F32).astype(BF16)


def _inproj_call(x2, mod, gain, w_main, seq):
    t, d = x2.shape
    tm = 512
    per_batch = seq // tm
    return pl.pallas_call(
        _inproj_kernel,
        grid=(PROJ_WIDTH // PROJ_N_TILE, t // tm),
        in_specs=[
            pl.BlockSpec((tm, d), lambda n, i: (i, 0)),
            pl.BlockSpec((1, N_MOD, d), lambda n, i: (i // per_batch, 0, 0)),
            pl.BlockSpec((1, d), lambda n, i: (0, 0)),
            pl.BlockSpec((d, PROJ_N_TILE), lambda n, i: (0, n)),
        ],
        out_specs=pl.BlockSpec((tm, PROJ_N_TILE), lambda n, i: (i, n)),
        out_shape=jax.ShapeDtypeStruct((t, PROJ_WIDTH), BF16),
        compiler_params=_params("arbitrary", "arbitrary"),
        name="in_proj",
    )(x2, mod, gain, w_main)


PARTS = 3
LR_STRIDE = 8
DT_ROWS = 2 * PARTS
X_BLOCKS = ("ev_fwd", "wv_fwd", "ev_bwd", "wv_bwd")
X_STRIDE = PARTS * SSM_HPG
LOG_DT_FLOOR = -200.0
LOG2_E = 1.4426950408889634
SSD_UNROLL = 2
GW_ALL = SSM_GROUPS * LANES


def _placement_constants():
    place = np.zeros((5 * PARTS, LANES, GW_ALL), np.float32)
    ones = np.zeros((2, 1, GW_ALL), np.float32)
    for g in range(SSM_GROUPS):
        for j in range(DT_COLS):
            src = g * DT_COLS + j
            base = g * LANES + j * LR_STRIDE
            for k in range(PARTS):
                place[k, src, base + k] = 1.0
                place[PARTS + k, src, base + PARTS + k] = -1.0
            place[2 * PARTS, src, base + DT_ROWS] = 1.0
            place[2 * PARTS + 1, src, base + DT_ROWS + 1] = 1.0
            ones[0, 0, base + PARTS:base + 2 * PARTS] = 1.0
            ones[1, 0, base:base + PARTS] = 1.0
        for r in range(SSM_HPG):
            fwd, bwd = g * DT_COLS + r, g * DT_COLS + SSM_HPG + r
            for k in range(PARTS):
                col = g * LANES + k * SSM_HPG + r
                place[3 * PARTS + k, fwd, col + 0 * X_STRIDE] = 1.0
                place[3 * PARTS + k, bwd, col + 2 * X_STRIDE] = 1.0
                place[4 * PARTS + k, fwd, col + 1 * X_STRIDE] = 1.0
                place[4 * PARTS + k, bwd, col + 3 * X_STRIDE] = 1.0
    return jnp.asarray(place, BF16), jnp.asarray(ones, F32)


def _expansion_constant():
    ex = np.zeros((LANES, len(X_BLOCKS) * GROUP_WIDTH), np.float32)
    for t in range(len(X_BLOCKS)):
        for k in range(PARTS):
            for r in range(SSM_HPG):
                row = t * X_STRIDE + k * SSM_HPG + r
                ex[row, t * GROUP_WIDTH + r * SSM_HEAD_DIM:t * GROUP_WIDTH + (r + 1) * SSM_HEAD_DIM] = 1.0
    return jnp.asarray(ex, BF16)


def _split3(x):
    hi = x.astype(BF16)
    r1 = x - hi.astype(F32)
    mid = r1.astype(BF16)
    lo = (r1 - mid.astype(F32)).astype(BF16)
    return hi, mid, lo


def _dt_kernel(x_ref, mod_ref, g_ref, wh_ref, wl_ref, bias_ref, alog_ref, place_ref, ones_ref,
               lmat_ref, rmat_ref, xmat_ref, dch_ref):
    q = SSM_CHUNK
    h = _prenorm_modulate(x_ref[...], g_ref[...], mod_ref[0, 0:1, :], mod_ref[0, 1:2, :])
    hb = h.astype(BF16)
    h_lo = (h - hb.astype(F32)).astype(BF16)
    raw = (jnp.dot(hb, wh_ref[...], preferred_element_type=F32) + jnp.dot(hb, wl_ref[...], preferred_element_type=F32)
           + jnp.dot(h_lo, wh_ref[...], preferred_element_type=F32)) + bias_ref[...]
    dt = jnp.maximum(raw, 0.0) + jnp.log1p(jnp.exp(-jnp.abs(raw)))
    da = dt * (-jnp.exp(alog_ref[...]))
    row = lax.broadcasted_iota(jnp.int32, (q, q), 0)
    col = lax.broadcasted_iota(jnp.int32, (q, q), 1)
    tri = jnp.where(row >= col, 1.0, 0.0).astype(BF16)
    prefix = None
    for part in _split3(da):
        term = jnp.dot(tri, part, preferred_element_type=F32)
        prefix = term if prefix is None else prefix + term
    tot = prefix[q - 1:q, :]
    lane = lax.broadcasted_iota(jnp.int32, (1, LANES), 1)
    is_fwd = (lane % DT_COLS) < SSM_HPG
    cum = jnp.where(is_fwd, prefix, tot - prefix + da)
    ev = jnp.exp(cum)
    wv = jnp.exp(tot - cum) * dt
    cum2 = cum * LOG2_E
    rsub = cum2 - jnp.maximum(jnp.log(dt), LOG_DT_FLOOR) * LOG2_E
    diag = jnp.maximum(jnp.log(dt + pltpu.roll(dt, SSM_HPG, 1)), LOG_DT_FLOOR) * LOG2_E

    def placed(first, value, n_parts=PARTS):
        acc = None
        for k, part in enumerate(_split3(value)[:n_parts]):
            term = jnp.dot(part, place_ref[first + k], preferred_element_type=F32)
            acc = term if acc is None else acc + term
        return acc

    lmat_ref[...] = (placed(0, cum2) + ones_ref[0]).astype(BF16)
    rmat_ref[...] = (placed(PARTS, rsub) + placed(2 * PARTS, diag, 2) + ones_ref[1]).T.astype(BF16)
    xmat_ref[...] = (placed(3 * PARTS, ev) + placed(4 * PARTS, wv)).astype(BF16)
    dch_ref[0, 0] = jnp.exp(tot)


def _dt_call(x2, mod, gain, w_dt, dt_bias, a_log, bsz, seq):
    t, d = x2.shape
    q = SSM_CHUNK
    nc = seq // q
    place, ones = _placement_constants()
    w_hi = w_dt.astype(BF16)
    return pl.pallas_call(
        _dt_kernel,
        grid=(bsz, nc),
        in_specs=[
            pl.BlockSpec((q, d), lambda b, c: (b * nc + c, 0)),
            pl.BlockSpec((1, N_MOD, d), lambda b, c: (b, 0, 0)),
            pl.BlockSpec((1, d), lambda b, c: (0, 0)),
            pl.BlockSpec((d, LANES), lambda b, c: (0, 0)),
            pl.BlockSpec((d, LANES), lambda b, c: (0, 0)),
            pl.BlockSpec((1, LANES), lambda b, c: (0, 0)),
            pl.BlockSpec((1, LANES), lambda b, c: (0, 0)),
            pl.BlockSpec(place.shape, lambda b, c: (0, 0, 0)),
            pl.BlockSpec(ones.shape, lambda b, c: (0, 0, 0)),
        ],
        out_specs=[
            pl.BlockSpec((q, GW_ALL), lambda b, c: (b * nc + c, 0)),
            pl.BlockSpec((GW_ALL, q), lambda b, c: (b, c)),
            pl.BlockSpec((q, GW_ALL), lambda b, c: (b * nc + c, 0)),
            pl.BlockSpec((1, 1, 1, LANES), lambda b, c: (b, c, 0, 0)),
        ],
        out_shape=[
            jax.ShapeDtypeStruct((t, GW_ALL), BF16),
            jax.ShapeDtypeStruct((bsz * GW_ALL, seq), BF16),
            jax.ShapeDtypeStruct((t, GW_ALL), BF16),
            jax.ShapeDtypeStruct((bsz, nc, 1, LANES), F32),
        ],
        compiler_params=_params("arbitrary", "arbitrary"),
        name="dt_prep",
    )(x2, mod, gain, w_hi, (w_dt - w_hi.astype(F32)).astype(BF16), dt_bias, a_log, place, ones)


def _head_rms_rope(t, gain, cos, sin, lane):
    lo = lane < HEAD_DIM
    sq = t * t
    ss_lo = jnp.sum(jnp.where(lo, sq, 0.0), axis=-1, keepdims=True)
    ss_hi = jnp.sum(jnp.where(lo, 0.0, sq), axis=-1, keepdims=True)
    r = jnp.where(lo, lax.rsqrt(ss_lo * (1.0 / HEAD_DIM) + NORM_EPS), lax.rsqrt(ss_hi * (1.0 / HEAD_DIM) + NORM_EPS))
    tn = t * r * gain
    first_half = (lane % HEAD_DIM) < ROPE_HALF
    partner = jnp.where(first_half, pltpu.roll(tn, LANES - ROPE_HALF, 1), pltpu.roll(tn, ROPE_HALF, 1))
    return tn * cos + partner * sin


def _attn_kernel(sink_ref, q_ref, k_ref, v_ref, cos_ref, sin_ref, qg_ref, kg_ref, o_ref, kpad, vpad):
    blk = ATTN_BLOCK
    s = q_ref.shape[0]
    nb = s // blk
    lane = lax.broadcasted_iota(jnp.int32, (1, LANES), 1)
    lo = lane < HEAD_DIM

    zero_blk = jnp.zeros((blk, LANES), BF16)
    for j in range(2 * N_KV_HEADS):
        kpad[j, 0:blk, :] = zero_blk
        kpad[j, blk + s:2 * blk + s, :] = zero_blk
        vpad[j, 0:blk, :] = zero_blk
        vpad[j, blk + s:2 * blk + s, :] = zero_blk

    def prep(c, carry):
        r0 = pl.multiple_of(c * blk, blk)
        rows = pl.ds(r0, blk)
        dst = pl.ds(r0 + blk, blk)
        cos = cos_ref[0, rows, :]
        sin = sin_ref[0, rows, :]
        for tpair in range(N_KV_HEADS // 2):
            ls = slice(tpair * LANES, (tpair + 1) * LANES)
            kr = _head_rms_rope(k_ref[rows, ls].astype(F32), kg_ref[...], cos, sin, lane)
            vv = v_ref[rows, ls].astype(F32)
            for src, store in ((kr, kpad), (vv, vpad)):
                even_lo = jnp.where(lo, src, 0.0)
                odd_hi = jnp.where(lo, 0.0, src)
                g0, g1 = 2 * tpair, 2 * tpair + 1
                store[2 * g0, dst, :] = even_lo.astype(BF16)
                store[2 * g0 + 1, dst, :] = pltpu.roll(even_lo, HEAD_DIM, 1).astype(BF16)
                store[2 * g1, dst, :] = pltpu.roll(odd_hi, HEAD_DIM, 1).astype(BF16)
                store[2 * g1 + 1, dst, :] = odd_hi.astype(BF16)
        return carry

    lax.fori_loop(0, nb, prep, 0)

    rowi = lax.broadcasted_iota(jnp.int32, (2 * blk, 3 * blk), 0) % blk
    coli = lax.broadcasted_iota(jnp.int32, (2 * blk, 3 * blk), 1)
    rel = coli - rowi
    in_band = (rel >= 0) & (rel <= 2 * WINDOW)
    top_rows = lax.broadcasted_iota(jnp.int32, (2 * blk, 1), 0) < blk
    scale = HEAD_DIM ** -0.5

    def qblock(n, carry):
        r0 = pl.multiple_of(n * blk, blk)
        rows = pl.ds(r0, blk)
        band = pl.ds(r0, 3 * blk)
        cos = cos_ref[0, rows, :]
        sin = sin_ref[0, rows, :]
        kidx = coli + (r0 - blk)
        bias = jnp.where(in_band & (kidx >= 0) & (kidx < s), 0.0, NEG_BIG)
        for g in range(N_KV_HEADS):
            qs = []
            for pair in (2 * g, 2 * g + 1):
                qt = q_ref[rows, pair * LANES:(pair + 1) * LANES].astype(F32)
                qs.append(_head_rms_rope(qt, qg_ref[...], cos, sin, lane) * scale)
            qq = jnp.concatenate(qs, axis=0).astype(BF16)
            acc = jnp.zeros((2 * blk, LANES), F32)
            for par in range(2):
                kb = kpad[2 * g + par, band, :]
                sc = lax.dot_general(qq, kb, (((1,), (1,)), ((), ())), preferred_element_type=F32) + bias
                sink = jnp.where(top_rows, sink_ref[4 * g + par], sink_ref[4 * g + 2 + par])
                m = jnp.maximum(jnp.max(sc, axis=-1, keepdims=True), sink)
                p = jnp.exp(sc - m)
                denom = jnp.sum(p, axis=-1, keepdims=True) + jnp.exp(sink - m)
                pv = jnp.dot(p.astype(BF16), vpad[2 * g + par, band, :], preferred_element_type=F32)
                acc = acc + pv * (1.0 / denom)
            o_ref[rows, (2 * g) * LANES:(2 * g + 1) * LANES] = acc[0:blk].astype(BF16)
            o_ref[rows, (2 * g + 1) * LANES:(2 * g + 2) * LANES] = acc[blk:2 * blk].astype(BF16)
        return carry

    lax.fori_loop(0, nb, qblock, 0)


def _attn_call(proj, cos_t, sin_t, q_gain, k_gain, sink, bsz, seq):
    t = proj.shape[0]
    return pl.pallas_call(
        _attn_kernel,
        grid_spec=pltpu.PrefetchScalarGridSpec(
            num_scalar_prefetch=1,
            grid=(bsz,),
            in_specs=[
                pl.BlockSpec((seq, ATTN_WIDTH), lambda b, sk: (b, COL_Q // ATTN_WIDTH)),
                pl.BlockSpec((seq, KV_WIDTH), lambda b, sk: (b, COL_K // KV_WIDTH)),
                pl.BlockSpec((seq, KV_WIDTH), lambda b, sk: (b, COL_V // KV_WIDTH)),
                pl.BlockSpec((1, seq, LANES), lambda b, sk: (b, 0, 0)),
                pl.BlockSpec((1, seq, LANES), lambda b, sk: (b, 0, 0)),
                pl.BlockSpec((1, LANES), lambda b, sk: (0, 0)),
                pl.BlockSpec((1, LANES), lambda b, sk: (0, 0)),
            ],
            out_specs=pl.BlockSpec((seq, ATTN_WIDTH), lambda b, sk: (b, 0)),
            scratch_shapes=[
                pltpu.VMEM((2 * N_KV_HEADS, seq + 2 * ATTN_BLOCK, LANES), BF16),
                pltpu.VMEM((2 * N_KV_HEADS, seq + 2 * ATTN_BLOCK, LANES), BF16),
            ],
        ),
        out_shape=jax.ShapeDtypeStruct((t, ATTN_WIDTH), BF16),
        compiler_params=_params("arbitrary"),
        name="window_attn",
    )(sink, proj, proj, proj, cos_t, sin_t, q_gain, k_gain)


def _ssd_kernel(dch_ref, xs_ref, b_ref, c_ref, z_ref, cwx_ref, cwb_ref, cwc_ref, cbx_ref, cbb_ref, cbc_ref,
                lmat_ref, rmat_ref, xmat_ref, ex_ref, dskip_ref, ng_ref, o_ref,
                upad, xs_c, b_c, c_c, y_acc, st, st_b, d_all, m_all):
    q = SSM_CHUNK
    s = xs_ref.shape[0]
    nc = s // q
    pad = 16
    halo = SSM_CONV // 2
    win = q + pad
    width = GROUP_WIDTH + 2 * SSM_STATE

    upad[0:pad, :] = jnp.zeros((pad, width), F32)
    upad[pad + s:2 * pad + s, :] = jnp.zeros((pad, width), F32)

    def fill(c, carry):
        r0 = pl.multiple_of(c * q, q)
        rows = pl.ds(r0, q)
        dst = pl.ds(r0 + pad, q)
        upad[dst, 0:GROUP_WIDTH] = xs_ref[rows, :].astype(F32)
        upad[dst, GROUP_WIDTH:GROUP_WIDTH + SSM_STATE] = b_ref[rows, :].astype(F32)
        upad[dst, GROUP_WIDTH + SSM_STATE:width] = c_ref[rows, :].astype(F32)
        return carry

    lax.fori_loop(0, nc, fill, 0)

    def conv_cols(r0, lo_col, n_col, w_ref, bias_ref, dst_ref):
        window = upad[pl.ds(r0 + pad - SUBLANES, win), lo_col:lo_col + n_col]
        acc = jnp.zeros((q, n_col), F32) + bias_ref[...]
        for k in range(SSM_CONV):
            shift = SUBLANES - halo + k
            acc = acc + pltpu.roll(window, win - shift, 0)[0:q] * w_ref[k:k + 1, :]
        dst_ref[pl.ds(r0, q), :] = acc * (1.0 / (1.0 + jnp.exp(-acc)))

    def conv(c, carry):
        r0 = pl.multiple_of(c * q, q)
        for j in range(GROUP_WIDTH // LANES):
            conv_cols(r0, j * LANES, LANES, cwx_ref.at[:, j * LANES:(j + 1) * LANES],
                      cbx_ref.at[:, j * LANES:(j + 1) * LANES], xs_c.at[:, j * LANES:(j + 1) * LANES])
        conv_cols(r0, GROUP_WIDTH, SSM_STATE, cwb_ref, cbb_ref, b_c)
        conv_cols(r0, GROUP_WIDTH + SSM_STATE, SSM_STATE, cwc_ref, cbc_ref, c_c)
        return carry

    lax.fori_loop(0, nc, conv, 0)

    row = lax.broadcasted_iota(jnp.int32, (q, q), 0)
    col = lax.broadcasted_iota(jnp.int32, (q, q), 1)
    below = row > col
    on_diag = row == col
    head_of_row = row // LR_STRIDE
    low_half = lax.broadcasted_iota(jnp.int32, (q, LANES), 1) < SSM_HEAD_DIM
    gw = GROUP_WIDTH

    def expand(xm, block):
        return jnp.dot(xm, ex_ref[:, block * gw:(block + 1) * gw], preferred_element_type=F32)

    def decay_rows(c, first):
        return jnp.concatenate(
            [jnp.full((1, SSM_HEAD_DIM), dch_ref[0, 0, c, first + h], F32) for h in range(SSM_HPG)], axis=-1)

    def rows_of(c):
        return pl.ds(pl.multiple_of(c * q, q), q)

    def pair_step(i, j, slot):
        d_scr, m_scr = d_all.at[slot], m_all.at[slot]
        ri, rj = rows_of(i), rows_of(j)
        xs_i, xs_j = xs_c[ri, :], xs_c[rj, :]
        b_i, b_j = b_c[ri, :].astype(BF16), b_c[rj, :].astype(BF16)
        c_i, c_j = c_c[ri, :].astype(BF16), c_c[rj, :].astype(BF16)
        lm, rm = lmat_ref[ri, :], rmat_ref[:, ri]
        xm_i, xm_j = xmat_ref[ri, :], xmat_ref[rj, :]
        state_f, state_b = st[...], st_b[...]
        zero_b = jnp.zeros((q, LANES), BF16)

        wv_f, wv_b = expand(xm_i, 1), expand(xm_j, 3)
        ev_f, ev_b = expand(xm_i, 0), expand(xm_j, 2)
        off_f = jnp.dot(c_i, state_f.astype(BF16), preferred_element_type=F32)
        off_b = jnp.dot(c_j, state_b.astype(BF16), preferred_element_type=F32)
        cb = lax.dot_general(c_i, b_i, (((1,), (1,)), ((), ())), preferred_element_type=F32)
        for h in range(SSM_HPG):
            rhs = jnp.concatenate([jnp.where(head_of_row == h, rm, zero_b),
                                   jnp.where(head_of_row == SSM_HPG + h, rm, zero_b)], axis=1)
            d_scr[:, 2 * h * q:2 * (h + 1) * q] = jnp.dot(lm, rhs, preferred_element_type=F32)

        upd_f = lax.dot_general(b_i, (xs_i * wv_f).astype(BF16), (((0,), (0,)), ((), ())), preferred_element_type=F32)
        upd_b = lax.dot_general(b_j, (xs_j * wv_b).astype(BF16), (((0,), (0,)), ((), ())), preferred_element_type=F32)
        st[...] = state_f * decay_rows(i, 0) + upd_f
        st_b[...] = state_b * decay_rows(j, SSM_HPG) + upd_b
        y_i = off_f * ev_f + xs_i * dskip_ref[...]
        y_j = off_b * ev_b

        for h in range(SSM_HPG):
            d_f = d_scr[:, 2 * h * q:(2 * h + 1) * q]
            d_b = d_scr[:, (2 * h + 1) * q:(2 * h + 2) * q]
            diag_row = (SSM_HPG + h) * LR_STRIDE + DT_ROWS
            diag = rm[diag_row:diag_row + 1, :].astype(F32) + rm[diag_row + 1:diag_row + 2, :].astype(F32)
            e = jnp.exp2(jnp.where(below, d_f, jnp.where(on_diag, diag, d_b)))
            m_scr[:, h * q:(h + 1) * q] = (cb * e).astype(BF16)
        xs_b16 = xs_i.astype(BF16)
        ys = []
        for pair in range(SSM_HPG // 2):
            xpair = xs_b16[:, pair * LANES:(pair + 1) * LANES]
            stacked = jnp.concatenate([jnp.where(low_half, xpair, zero_b), jnp.where(low_half, zero_b, xpair)], axis=0)
            ys.append(jnp.dot(m_scr[:, 2 * pair * q:2 * (pair + 1) * q], stacked, preferred_element_type=F32))
        return y_i + jnp.concatenate(ys, axis=-1), y_j

    def finish(c, y):
        rows = rows_of(c)
        zz = z_ref[rows, :].astype(F32)
        y = y * (zz * (1.0 / (1.0 + jnp.exp(-zz))))
        ms = jnp.mean(y * y, axis=-1, keepdims=True)
        o_ref[rows, :] = (y * lax.rsqrt(ms + NORM_EPS) * ng_ref[...]).astype(BF16)

    st[...] = jnp.zeros_like(st)
    st_b[...] = jnp.zeros_like(st_b)

    unroll = SSD_UNROLL

    def outward(t, carry):
        for u in range(unroll):
            i = t * unroll + u
            j = nc - 1 - i
            y_i, y_j = pair_step(i, j, u)
            y_acc[rows_of(i), :] = y_i
            y_acc[rows_of(j), :] = y_j
        return carry

    def inward(t, carry):
        for u in range(unroll):
            i = t * unroll + u
            j = nc - 1 - i
            y_i, y_j = pair_step(i, j, u)
            finish(i, y_acc[rows_of(i), :] + y_i)
            finish(j, y_acc[rows_of(j), :] + y_j)
        return carry

    lax.fori_loop(0, nc // (2 * unroll), outward, 0)
    lax.fori_loop(nc // (2 * unroll), nc // unroll, inward, 0)


def _ssd_call(proj, conv_w, conv_b, lmat, rmat, xmat, dch, d_skip, norm_g, bsz, seq):
    t = proj.shape[0]
    g_n = SSM_GROUPS
    q = SSM_CHUNK
    gw = GROUP_WIDTH
    width = GROUP_WIDTH + 2 * SSM_STATE
    ex = _expansion_constant()
    return pl.pallas_call(
        _ssd_kernel,
        grid=(bsz, g_n),
        in_specs=[
            pl.BlockSpec((1, 1, seq // q, DT_COLS), lambda b, g: (b, g, 0, 0), memory_space=pltpu.SMEM),
            pl.BlockSpec((seq, gw), lambda b, g: (b, COL_XS // gw + g)),
            pl.BlockSpec((seq, SSM_STATE), lambda b, g: (b, COL_B // SSM_STATE + g)),
            pl.BlockSpec((seq, SSM_STATE), lambda b, g: (b, COL_C // SSM_STATE + g)),
            pl.BlockSpec((seq, gw), lambda b, g: (b, COL_Z // gw + g)),
            pl.BlockSpec((SSM_CONV, gw), lambda b, g: (0, g)),
            pl.BlockSpec((SSM_CONV, SSM_STATE), lambda b, g: (0, SSM_INNER // SSM_STATE + g)),
            pl.BlockSpec((SSM_CONV, SSM_STATE), lambda b, g: (0, SSM_INNER // SSM_STATE + g_n + g)),
            pl.BlockSpec((1, gw), lambda b, g: (0, g)),
            pl.BlockSpec((1, SSM_STATE), lambda b, g: (0, SSM_INNER // SSM_STATE + g)),
            pl.BlockSpec((1, SSM_STATE), lambda b, g: (0, SSM_INNER // SSM_STATE + g_n + g)),
            pl.BlockSpec((seq, LANES), lambda b, g: (b, g)),
            pl.BlockSpec((LANES, seq), lambda b, g: (b * g_n + g, 0)),
            pl.BlockSpec((seq, LANES), lambda b, g: (b, g)),
            pl.BlockSpec(ex.shape, lambda b, g: (0, 0)),
            pl.BlockSpec((1, gw), lambda b, g: (0, g)),
            pl.BlockSpec((1, gw), lambda b, g: (0, g)),
        ],
        out_specs=pl.BlockSpec((seq, gw), lambda b, g: (b, g)),
        out_shape=jax.ShapeDtypeStruct((t, SSM_INNER), BF16),
        scratch_shapes=[
            pltpu.VMEM((seq + 32, width), F32),
            pltpu.VMEM((seq, gw), F32),
            pltpu.VMEM((seq, SSM_STATE), F32),
            pltpu.VMEM((seq, SSM_STATE), F32),
            pltpu.VMEM((seq, gw), F32),
            pltpu.VMEM((SSM_STATE, gw), F32),
            pltpu.VMEM((SSM_STATE, gw), F32),
            pltpu.VMEM((SSD_UNROLL, q, 2 * SSM_HPG * q), F32),
            pltpu.VMEM((SSD_UNROLL, q, SSM_HPG * q), BF16),
        ],
        compiler_params=_params("arbitrary", "arbitrary"),
        name="ssd_mixer",
    )(dch, proj, proj, proj, proj, conv_w, conv_w, conv_w, conv_b, conv_b, conv_b,
      lmat, rmat, xmat, ex, d_skip, norm_g)


def _outproj_kernel(attn_ref, ssm_ref, ga_ref, gs_ref, x_ref, mod_ref, wa_ref, ws_ref, wo_ref, o_ref):
    ya = jnp.dot(attn_ref[...], wa_ref[...], preferred_element_type=F32)
    ys = jnp.dot(ssm_ref[...], ws_ref[...], preferred_element_type=F32)
    ga = ga_ref[...].astype(F32)
    gs = gs_ref[...].astype(F32)
    merged = ya * (1.0 / (1.0 + jnp.exp(-ga))) + ys * (1.0 / (1.0 + jnp.exp(-gs)))
    y = jnp.dot(merged.astype(BF16), wo_ref[...], preferred_element_type=F32)
    o_ref[...] = x_ref[...] + mod_ref[0, 2:3, :] * y


def _outproj_call(attn, ssm, proj, x2, mod, w_attn_o, w_ssm_o, w_out, seq):
    t, d = x2.shape
    tm = 512
    per_batch = seq // tm
    const = lambda i: (0, 0)
    return pl.pallas_call(
        _outproj_kernel,
        grid=(t // tm,),
        in_specs=[
            pl.BlockSpec((tm, ATTN_WIDTH), lambda i: (i, 0)),
            pl.BlockSpec((tm, SSM_INNER), lambda i: (i, 0)),
            pl.BlockSpec((tm, d), lambda i: (i, COL_GATE_ATTN // d)),
            pl.BlockSpec((tm, d), lambda i: (i, COL_GATE_SSM // d)),
            pl.BlockSpec((tm, d), lambda i: (i, 0)),
            pl.BlockSpec((1, N_MOD, d), lambda i: (i // per_batch, 0, 0)),
            pl.BlockSpec((ATTN_WIDTH, d), const),
            pl.BlockSpec((SSM_INNER, d), const),
            pl.BlockSpec((d, d), const),
        ],
        out_specs=pl.BlockSpec((tm, d), lambda i: (i, 0)),
        out_shape=jax.ShapeDtypeStruct((t, d), F32),
        compiler_params=_params("arbitrary"),
        name="out_proj",
    )(attn, ssm, proj, proj, x2, mod, w_attn_o, w_ssm_o, w_out)


def _permute_in_proj(w_in):
    w_main = jnp.concatenate(
        [w_in[:, _O_Q:_O_K], w_in[:, _O_GATES:], w_in[:, _O_Z:_O_XBC], w_in[:, _O_XBC:_O_XBC + SSM_INNER],
         w_in[:, _O_K:_O_Z], w_in[:, _O_XBC + SSM_INNER:_O_DT]], axis=1).astype(BF16)
    return w_main


def _dt_permutation():
    idx = np.zeros(2 * SSM_HEADS, np.int32)
    for g in range(SSM_GROUPS):
        for d in range(2):
            for r in range(SSM_HPG):
                idx[g * DT_COLS + d * SSM_HPG + r] = d * SSM_HEADS + g * SSM_HPG + r
    return idx


def _pad_lanes(a):
    return jnp.pad(a, [(0, 0)] * (a.ndim - 1) + [(0, LANES - a.shape[-1])])


def _mixer_layer(x2, mod, cos_t, sin_t, bsz, seq, norm_g, w_in, q_norm_g, k_norm_g, attn_sink, conv_w, conv_b,
                 a_log, dt_bias, ssm_d, ssm_norm_g, w_attn_o, w_ssm_o, w_out):
    d = x2.shape[1]
    gain = norm_g.reshape(1, d)
    perm = _dt_permutation()
    w_dt = _pad_lanes(w_in[:, _O_DT:_O_GATES][:, perm])
    dt_b = _pad_lanes(dt_bias.reshape(-1)[perm].reshape(1, -1))
    a_lg = _pad_lanes(a_log.reshape(-1)[perm].reshape(1, -1))

    proj = _inproj_call(x2, mod, gain, _permute_in_proj(w_in), seq)
    lmat, rmat, xmat, dch = _dt_call(x2, mod, gain, w_dt, dt_b, a_lg, bsz, seq)
    nc = seq // SSM_CHUNK
    dch = dch[:, :, 0, :2 * SSM_HEADS].reshape(bsz, nc, SSM_GROUPS, DT_COLS).transpose(0, 2, 1, 3)

    attn = _attn_call(proj, cos_t, sin_t, jnp.tile(q_norm_g, 2).reshape(1, LANES),
                      jnp.tile(k_norm_g, 2).reshape(1, LANES), attn_sink, bsz, seq)
    ssm = _ssd_call(proj, conv_w, conv_b.reshape(1, -1), lmat, rmat, xmat, dch,
                    jnp.repeat(ssm_d, SSM_HEAD_DIM).reshape(1, SSM_INNER), ssm_norm_g.reshape(1, SSM_INNER),
                    bsz, seq)
    return _outproj_call(attn, ssm, proj, x2, mod, w_attn_o.astype(BF16), w_ssm_o.astype(BF16),
                         w_out.astype(BF16), seq)


ROUTE_BLOCK = 256
SLAB = SUBLANES
LOCAL_ROWS = ROUTE_BLOCK * TOP_K + ROUTE_BLOCK
LOCAL_SLABS = LOCAL_ROWS // SLAB
TILE_ROWS = 512
TILE_SLABS = TILE_ROWS // SLAB


def _route_kernel(x_ref, mod_ref, g_ref, rwh_ref, rwl_ref, rb_ref, xloc_ref, tokrow_ref, colw_ref, cnt_ref):
    tb = ROUTE_BLOCK
    h = _prenorm_modulate(x_ref[...], g_ref[...], mod_ref[0, 3:4, :], mod_ref[0, 4:5, :])
    hb = h.astype(BF16)
    h_lo = (h - hb.astype(F32)).astype(BF16)
    logits = (jnp.dot(hb, rwh_ref[...], preferred_element_type=F32)
              + jnp.dot(hb, rwl_ref[...], preferred_element_type=F32)
              + jnp.dot(h_lo, rwh_ref[...], preferred_element_type=F32)) + rb_ref[...]
    v = logits.T[0:N_EXPERTS, :]
    erow = lax.broadcasted_iota(jnp.int32, (N_EXPERTS, tb), 0)
    hots, tops = [], []
    for _ in range(TOP_K):
        mk = jnp.max(v, axis=0, keepdims=True)
        first = jnp.min(jnp.where(v == mk, erow, N_EXPERTS), axis=0, keepdims=True)
        hot = erow == first
        v = jnp.where(hot, -jnp.inf, v)
        hots.append(hot)
        tops.append(mk)
    ps = [jnp.exp(mk - tops[0]) for mk in tops]
    denom = ps[0] + ps[1] + ps[2] + ps[3]
    sel = jnp.zeros((N_EXPERTS, tb), F32)
    for hot in hots:
        sel = sel + jnp.where(hot, 1.0, 0.0)
    ti = lax.broadcasted_iota(jnp.int32, (tb, tb), 0)
    tj = lax.broadcasted_iota(jnp.int32, (tb, tb), 1)
    before = jnp.where(ti < tj, 1.0, 0.0).astype(BF16)
    rank = jnp.dot(sel.astype(BF16), before, preferred_element_type=F32)
    count = jnp.sum(sel, axis=1, keepdims=True)
    slabs = jnp.floor((count + (SLAB - 1)) * (1.0 / SLAB))
    ei = lax.broadcasted_iota(jnp.int32, (N_EXPERTS, N_EXPERTS), 0)
    ej = lax.broadcasted_iota(jnp.int32, (N_EXPERTS, N_EXPERTS), 1)
    earlier = jnp.where(ej < ei, 1.0, 0.0).astype(BF16)
    slabs_b = jnp.broadcast_to(slabs, (N_EXPERTS, tb))
    start = jnp.dot(earlier, slabs_b.astype(BF16), preferred_element_type=F32)
    dest = start * SLAB + rank
    dki = [jnp.sum(jnp.where(hot, dest, 0.0), axis=0, keepdims=True).astype(jnp.int32) for hot in hots]
    wks = [p / denom for p in ps]
    cnt_ref[0] = slabs_b[:, 0:LANES]
    tok1 = (lax.broadcasted_iota(jnp.int32, (SUBLANES, tb), 1) + 1).astype(BF16)
    for c in range(LOCAL_ROWS // tb):
        rs = slice(c * tb, (c + 1) * tb)
        ri = lax.broadcasted_iota(jnp.int32, (tb, tb), 0) + c * tb
        weighted = jnp.where(ri == dki[0], wks[0], jnp.where(ri == dki[1], wks[1], jnp.where(
            ri == dki[2], wks[2], jnp.where(ri == dki[3], wks[3], 0.0))))
        onehot = jnp.where(weighted > 0.0, 1.0, 0.0).astype(BF16)
        xloc_ref[0, rs, :] = jnp.dot(onehot, hb, preferred_element_type=F32)
        colw_ref[0, rs, :] = jnp.broadcast_to(jnp.sum(weighted, axis=1, keepdims=True), (tb, LANES))
        tokrow_ref[0, :, rs] = lax.dot_general(tok1, onehot, (((1,), (1,)), ((), ())), preferred_element_type=F32)


def _route_call(x2, mod, gain, router_w, router_b, seq):
    t, d = x2.shape
    tb = ROUTE_BLOCK
    nblk = t // tb
    per_batch = seq // tb
    rw = _pad_lanes(router_w)
    rw_hi = rw.astype(BF16)
    return pl.pallas_call(
        _route_kernel,
        grid=(nblk,),
        in_specs=[
            pl.BlockSpec((tb, d), lambda i: (i, 0)),
            pl.BlockSpec((1, N_MOD, d), lambda i: (i // per_batch, 0, 0)),
            pl.BlockSpec((1, d), lambda i: (0, 0)),
            pl.BlockSpec((d, LANES), lambda i: (0, 0)),
            pl.BlockSpec((d, LANES), lambda i: (0, 0)),
            pl.BlockSpec((1, LANES), lambda i: (0, 0)),
        ],
        out_specs=[
            pl.BlockSpec((1, LOCAL_ROWS, d), lambda i: (i, 0, 0)),
            pl.BlockSpec((1, SUBLANES, LOCAL_ROWS), lambda i: (i, 0, 0)),
            pl.BlockSpec((1, LOCAL_ROWS, LANES), lambda i: (i, 0, 0)),
            pl.BlockSpec((1, N_EXPERTS, LANES), lambda i: (i, 0, 0)),
        ],
        out_shape=[
            jax.ShapeDtypeStruct((nblk, LOCAL_ROWS, d), F32),
            jax.ShapeDtypeStruct((nblk, SUBLANES, LOCAL_ROWS), F32),
            jax.ShapeDtypeStruct((nblk, LOCAL_ROWS, LANES), F32),
            jax.ShapeDtypeStruct((nblk, N_EXPERTS, LANES), F32),
        ],
        compiler_params=_params("arbitrary"),
        name="moe_route",
    )(x2, mod, gain, rw_hi, (rw - rw_hi.astype(F32)).astype(BF16), _pad_lanes(router_b.reshape(1, -1)))


def _slab_plan(slab_counts):
    nblk = slab_counts.shape[0]
    max_slabs = nblk * LOCAL_SLABS + N_EXPERTS * (TILE_SLABS - 1)
    max_tiles = -(-max_slabs // TILE_SLABS)
    c8 = slab_counts.astype(jnp.int32)
    local_start = jnp.cumsum(c8, axis=1) - c8
    per_expert = jnp.sum(c8, axis=0)
    tiles_e = (per_expert + TILE_SLABS - 1) // TILE_SLABS
    tile_start = jnp.cumsum(tiles_e) - tiles_e
    num_tiles = jnp.sum(tiles_e)
    expert_start = tile_start * TILE_SLABS
    block_off = jnp.cumsum(c8, axis=0) - c8
    seg_start = expert_start[None, :] + block_off

    sl = jnp.arange(LOCAL_SLABS, dtype=jnp.int32)[None, :, None]
    in_seg = (local_start[:, None, :] <= sl) & (sl < (local_start + c8)[:, None, :])
    slab_pos = jnp.sum(jnp.where(in_seg, (seg_start - local_start)[:, None, :] + sl, 0), axis=-1).reshape(-1)

    p = jnp.arange(max_tiles * TILE_SLABS, dtype=jnp.int32)[:, None]
    in_exp = (expert_start[None, :] <= p) & (p < (expert_start + per_expert)[None, :])
    off = p - jnp.sum(jnp.where(in_exp, expert_start[None, :], 0), axis=-1, keepdims=True)
    table = jnp.concatenate([block_off.T, c8.T, local_start.T], axis=1).astype(F32)
    picked = jnp.round(jnp.dot(in_exp.astype(F32), table, precision=HIGHEST)).astype(jnp.int32)
    boff, cnt, lst = picked[:, :nblk], picked[:, nblk:2 * nblk], picked[:, 2 * nblk:]
    in_blk = (boff <= off) & (off < boff + cnt)
    blk_base = jnp.arange(nblk, dtype=jnp.int32)[None, :] * LOCAL_SLABS
    slab_src = jnp.sum(jnp.where(in_blk, blk_base + lst - boff + off, 0), axis=-1)

    ti = jnp.arange(max_tiles, dtype=jnp.int32)
    tile_expert = jnp.clip(jnp.sum(tile_start[None, :] <= jnp.minimum(ti, num_tiles - 1)[:, None], axis=-1) - 1,
                           0, N_EXPERTS - 1)
    return tile_expert.astype(jnp.int32), slab_src.astype(jnp.int32), slab_pos.astype(jnp.int32), \
        num_tiles.reshape(1).astype(jnp.int32), max_tiles


def _expert_kernel(te_ref, src_ref, nt_ref, xloc_hbm, wg_ref, bg_ref, wu_ref, bu_ref, wd_ref, bd_ref, y_ref,
                   xbuf, sem, wg_s, wu_s, wd_s):
    i = pl.program_id(0)
    nt = nt_ref[0]

    def slab_copy(tile, slot, j):
        return pltpu.make_async_copy(xloc_hbm.at[src_ref[tile * TILE_SLABS + j]], xbuf.at[slot, j], sem.at[slot])

    def issue(tile, slot):
        for j in range(TILE_SLABS):
            slab_copy(tile, slot, j).start()

    @pl.when(i == 0)
    def _():
        issue(0, 0)

    @pl.when(i + 1 < nt)
    def _():
        issue(i + 1, (i + 1) % 2)

    @pl.when(i < nt)
    def _():
        slot = i % 2
        for j in range(TILE_SLABS):
            slab_copy(i, slot, j).wait()
        new_expert = jnp.logical_or(i == 0, te_ref[i] != te_ref[jnp.maximum(i - 1, 0)])

        @pl.when(new_expert)
        def _():
            rows = 128
            for src, dst in ((wg_ref, wg_s), (wu_ref, wu_s), (wd_ref, wd_s)):
                def cast(c, carry, src=src, dst=dst):
                    r = pl.ds(pl.multiple_of(c * rows, rows), rows)
                    dst[r, :] = src[0, 0, r, :].astype(BF16)
                    return carry
                lax.fori_loop(0, src.shape[2] // rows, cast, 0)

        x = xbuf[slot].reshape(TILE_ROWS, xbuf.shape[-1]).astype(BF16)
        gate = jnp.dot(x, wg_s[...], preferred_element_type=F32) + bg_ref[0, 0]
        up = jnp.dot(x, wu_s[...], preferred_element_type=F32) + bu_ref[0, 0]
        glu = jnp.minimum(gate, SWIGLU_LIMIT)
        lin = jnp.clip(up, -SWIGLU_LIMIT, SWIGLU_LIMIT)
        act = glu * (1.0 / (1.0 + jnp.exp(-SWIGLU_ALPHA * glu))) * (lin + 1.0)
        y_ref[...] = jnp.dot(act.astype(BF16), wd_s[...], preferred_element_type=F32) + bd_ref[0, 0]

    @pl.when(i >= nt)
    def _():
        y_ref[...] = jnp.zeros_like(y_ref)


def _expert_call(tile_expert, slab_src, num_tiles, max_tiles, xloc, layer, w_gate, b_gate, w_up, b_up, w_down, b_down):
    nblk, _, d = xloc.shape
    depth, _, _, ff = w_gate.shape
    xloc3 = xloc.reshape(nblk * LOCAL_SLABS, SLAB, d)
    wspec = lambda k, n: pl.BlockSpec((1, 1, k, n), lambda i, te, ss, nt: (layer, te[i], 0, 0))
    return pl.pallas_call(
        _expert_kernel,
        grid_spec=pltpu.PrefetchScalarGridSpec(
            num_scalar_prefetch=3,
            grid=(max_tiles,),
            in_specs=[
                pl.BlockSpec(memory_space=pl.ANY),
                wspec(d, ff), wspec(1, ff), wspec(d, ff), wspec(1, ff), wspec(ff, d), wspec(1, d),
            ],
            out_specs=pl.BlockSpec((TILE_ROWS, d), lambda i, te, ss, nt: (i, 0)),
            scratch_shapes=[
                pltpu.VMEM((2, TILE_SLABS, SLAB, d), F32),
                pltpu.SemaphoreType.DMA((2,)),
                pltpu.VMEM((d, ff), BF16),
                pltpu.VMEM((d, ff), BF16),
                pltpu.VMEM((ff, d), BF16),
            ],
        ),
        out_shape=jax.ShapeDtypeStruct((max_tiles * TILE_ROWS, d), F32),
        compiler_params=_params("arbitrary"),
        name="moe_experts",
    )(tile_expert, slab_src, num_tiles, xloc3, w_gate, b_gate.reshape(depth, N_EXPERTS, 1, ff), w_up,
      b_up.reshape(depth, N_EXPERTS, 1, ff), w_down, b_down.reshape(depth, N_EXPERTS, 1, d))


COMBINE_ISSUE_UNROLL = 8


def _combine_kernel(pos_ref, y_hbm, tokrow_ref, colw_ref, x_ref, mod_ref, o_ref, ybuf, sem):
    b = pl.program_id(0)
    nb = pl.num_programs(0)
    tb = ROUTE_BLOCK
    d = x_ref.shape[-1]

    def slab_copy(blk, slot, j):
        return pltpu.make_async_copy(y_hbm.at[pos_ref[blk * LOCAL_SLABS + j]], ybuf.at[slot, j], sem.at[slot])

    def issue(blk, slot):
        def body(jj, carry):
            for u in range(COMBINE_ISSUE_UNROLL):
                slab_copy(blk, slot, jj * COMBINE_ISSUE_UNROLL + u).start()
            return carry
        lax.fori_loop(0, LOCAL_SLABS // COMBINE_ISSUE_UNROLL, body, 0)

    @pl.when(b == 0)
    def _():
        issue(0, 0)

    @pl.when(b + 1 < nb)
    def _():
        issue(b + 1, (b + 1) % 2)

    slot = b % 2
    pltpu.make_async_copy(y_hbm.at[pl.ds(0, LOCAL_SLABS)], ybuf.at[slot], sem.at[slot]).wait()

    tok1 = (lax.broadcasted_iota(jnp.int32, (tb, tb), 0) + 1).astype(F32)
    acc = jnp.zeros((tb, d), F32)
    slabs_per_chunk = tb // SLAB
    for c in range(LOCAL_ROWS // tb):
        rs = slice(c * tb, (c + 1) * tb)
        y = ybuf[slot, c * slabs_per_chunk:(c + 1) * slabs_per_chunk].reshape(tb, d)
        w = colw_ref[0, rs, :]
        yw = jnp.concatenate([y[:, t * LANES:(t + 1) * LANES] * w for t in range(d // LANES)], axis=1).astype(BF16)
        unperm = jnp.where(tok1 == tokrow_ref[0, 0:1, rs], 1.0, 0.0).astype(BF16)
        acc = acc + jnp.dot(unperm, yw, preferred_element_type=F32)
    o_ref[...] = x_ref[...] + mod_ref[0, 5:6, :] * acc


def _combine_call(slab_pos, y, tokrow, colw, x2, mod, seq):
    t, d = x2.shape
    tb = ROUTE_BLOCK
    nblk = t // tb
    per_batch = seq // tb
    y3 = y.reshape(y.shape[0] // SLAB, SLAB, d)
    return pl.pallas_call(
        _combine_kernel,
        grid_spec=pltpu.PrefetchScalarGridSpec(
            num_scalar_prefetch=1,
            grid=(nblk,),
            in_specs=[
                pl.BlockSpec(memory_space=pl.ANY),
                pl.BlockSpec((1, SUBLANES, LOCAL_ROWS), lambda i, sp: (i, 0, 0)),
                pl.BlockSpec((1, LOCAL_ROWS, LANES), lambda i, sp: (i, 0, 0)),
                pl.BlockSpec((tb, d), lambda i, sp: (i, 0)),
                pl.BlockSpec((1, N_MOD, d), lambda i, sp: (i // per_batch, 0, 0)),
            ],
            out_specs=pl.BlockSpec((tb, d), lambda i, sp: (i, 0)),
            scratch_shapes=[
                pltpu.VMEM((2, LOCAL_SLABS, SLAB, d), F32),
                pltpu.SemaphoreType.DMA((2,)),
            ],
        ),
        out_shape=jax.ShapeDtypeStruct((t, d), F32),
        compiler_params=_params("arbitrary"),
        name="moe_combine",
    )(slab_pos, y3, tokrow, colw, x2, mod)


def _moe_layer(x2, mod, seq, layer, norm_g, router_w, router_b, w_gate, b_gate, w_up, b_up, w_down, b_down):
    d = x2.shape[1]
    xloc, tokrow, colw, cnt = _route_call(x2, mod, norm_g.reshape(1, d), router_w, router_b, seq)
    tile_expert, slab_src, slab_pos, num_tiles, max_tiles = _slab_plan(cnt[:, :, 0])
    y = _expert_call(tile_expert, slab_src, num_tiles, max_tiles, xloc, layer, w_gate, b_gate, w_up, b_up, w_down,
                     b_down)
    return _combine_call(slab_pos, y, tokrow, colw, x2, mod, seq)


def kernel(x, c, positions, ada_w, ada_b, norm1_g, norm2_g, w_in, q_norm_g, k_norm_g, attn_sink, conv_w, conv_b,
           a_log, dt_bias, ssm_d, ssm_norm_g, w_attn_o, w_ssm_o, w_out, router_w, router_b, exp_w_gate, exp_b_gate,
           exp_w_up, exp_b_up, exp_w_down, exp_b_down):
    bsz, seq, d = x.shape
    depth = ada_w.shape[0]
    assert d == D_MODEL and seq % 512 == 0 and (bsz * seq) % ROUTE_BLOCK == 0
    mod_all = _ada_call(c, ada_w, ada_b).reshape(depth, bsz, N_MOD, d)
    cos_t, sin_t = _rope_call(positions)
    x2 = x.reshape(bsz * seq, d)
    for l in range(depth):
        mod = mod_all[l]
        x2 = _mixer_layer(x2, mod, cos_t, sin_t, bsz, seq, norm1_g[l], w_in[l], q_norm_g[l], k_norm_g[l],
                          attn_sink[l], conv_w[l], conv_b[l], a_log[l], dt_bias[l], ssm_d[l], ssm_norm_g[l],
                          w_attn_o[l], w_ssm_o[l], w_out[l])
        x2 = _moe_layer(x2, mod, seq, l, norm2_g[l], router_w[l], router_b[l], exp_w_gate, exp_b_gate,
                        exp_w_up, exp_b_up, exp_w_down, exp_b_down)
    return x2.reshape(bsz, seq, d)
```

```python
import functools

import jax
import jax.numpy as jnp
import numpy as np
from jax import lax
from jax.experimental import pallas as pl
from jax.experimental.pallas import tpu as pltpu

F32 = jnp.float32
BF16 = jnp.bfloat16
HIGHEST = lax.Precision.HIGHEST

LANES = 128
SUBLANES = 8
VMEM_LIMIT = 56 * 1024 * 1024

D_MODEL = 1024
NORM_EPS = 1e-5
N_MOD = 6

HEAD_DIM = 64
N_Q_HEADS = 16
N_KV_HEADS = 4
ATTN_WIDTH = N_Q_HEADS * HEAD_DIM
KV_WIDTH = N_KV_HEADS * HEAD_DIM
WINDOW = 128
ATTN_BLOCK = 128
ROPE_THETA = 500000.0
ROPE_DIMS = HEAD_DIM // 4
ROPE_HALF = ROPE_DIMS // 2
NEG_BIG = -1e30

SSM_INNER = 2 * D_MODEL
SSM_HEAD_DIM = 64
SSM_HEADS = SSM_INNER // SSM_HEAD_DIM
SSM_GROUPS = 4
SSM_HPG = SSM_HEADS // SSM_GROUPS
SSM_STATE = 128
SSM_CONV = 5
SSM_CHUNK = 128
GROUP_WIDTH = SSM_INNER // SSM_GROUPS
DT_COLS = 2 * SSM_HPG

N_EXPERTS = 32
TOP_K = 4
SWIGLU_LIMIT = 7.0
SWIGLU_ALPHA = 1.702

COL_Q = 0
COL_GATE_ATTN = 1024
COL_GATE_SSM = 2048
COL_Z = 3072
COL_XS = 5120
COL_K = 7168
COL_V = 7424
COL_B = 7680
COL_C = 8192
PROJ_WIDTH = 8704
PROJ_N_TILE = PROJ_WIDTH // 4

_O_Q, _O_K, _O_V, _O_Z, _O_XBC, _O_DT, _O_GATES = 0, 1024, 1280, 1536, 3584, 6656, 6720


def _params(*sem):
    return pltpu.CompilerParams(dimension_semantics=sem, vmem_limit_bytes=VMEM_LIMIT)


def _prenorm_modulate(x, gain, shift, scale):
    ms = jnp.mean(x * x, axis=-1, keepdims=True)
    return (x * lax.rsqrt(ms + NORM_EPS) * gain) * (1.0 + scale) + shift


def _ada_kernel(c_ref, w_ref, b_ref, o_ref):
    c = c_ref[...]
    c_act = c * (1.0 / (1.0 + jnp.exp(-c)))
    o_ref[0] = jnp.dot(c_act, w_ref[0], precision=HIGHEST, preferred_element_type=F32) + b_ref[0]


def _ada_call(c, ada_w, ada_b):
    depth, d, n = ada_w.shape
    bsz = c.shape[0]
    tn = 1536
    return pl.pallas_call(
        _ada_kernel,
        grid=(depth, n // tn),
        in_specs=[
            pl.BlockSpec((bsz, d), lambda l, j: (0, 0)),
            pl.BlockSpec((1, d, tn), lambda l, j: (l, 0, j)),
            pl.BlockSpec((1, 1, tn), lambda l, j: (l, 0, j)),
        ],
        out_specs=pl.BlockSpec((1, bsz, tn), lambda l, j: (l, 0, j)),
        out_shape=jax.ShapeDtypeStruct((depth, bsz, n), F32),
        compiler_params=_params("arbitrary", "arbitrary"),
        name="ada_mod",
    )(c, ada_w, ada_b.reshape(depth, 1, n))


def _rope_kernel(pos_ref, freq_ref, sign_ref, cos_ref, sin_ref):
    ang = pos_ref[0].astype(F32) * freq_ref[...]
    sign = sign_ref[...]
    cos_ref[0] = jnp.where(sign == 0.0, 1.0, jnp.cos(ang))
    sin_ref[0] = jnp.sin(ang) * sign


def _rope_call(positions):
    bsz, s = positions.shape
    lane = np.arange(LANES) % HEAD_DIM
    inv_freq = ROPE_THETA ** (-np.arange(0, ROPE_DIMS, 2, dtype=np.float32) / ROPE_DIMS)
    freq = np.where(lane < ROPE_DIMS, inv_freq[lane % ROPE_HALF], 0.0).astype(np.float32)
    sign = np.where(lane < ROPE_HALF, -1.0, np.where(lane < ROPE_DIMS, 1.0, 0.0)).astype(np.float32)
    ts = 512
    spec = pl.BlockSpec((1, ts, LANES), lambda b, i: (b, i, 0))
    return pl.pallas_call(
        _rope_kernel,
        grid=(bsz, s // ts),
        in_specs=[
            pl.BlockSpec((1, ts, 1), lambda b, i: (b, i, 0)),
            pl.BlockSpec((1, LANES), lambda b, i: (0, 0)),
            pl.BlockSpec((1, LANES), lambda b, i: (0, 0)),
        ],
        out_specs=[spec, spec],
        out_shape=[jax.ShapeDtypeStruct((bsz, s, LANES), F32)] * 2,
        compiler_params=_params("arbitrary", "arbitrary"),
        name="rope_tables",
    )(positions.reshape(bsz, s, 1), jnp.asarray(freq).reshape(1, LANES), jnp.asarray(sign).reshape(1, LANES))


def _inproj_kernel(x_ref, mod_ref, g_ref, w_ref, o_ref, h_scr):
    @pl.when(pl.program_id(1) == 0)
    def _():
        rows = 256
        def fill(c, carry):
            r = pl.ds(pl.multiple_of(c * rows, rows), rows)
            h_scr[r, :] = _prenorm_modulate(x_ref[r, :], g_ref[...], mod_ref[0, 0:1, :], mod_ref[0, 1:2, :]).astype(BF16)
            return carry
        lax.fori_loop(0, h_scr.shape[0] // rows, fill, 0)

    o_ref[...] = jnp.dot(h_scr[...], w_ref[...], preferred_element_type=F32).astype(BF16)


def _inproj_call(x2, mod, gain, w_main, seq):
    t, d = x2.shape
    tm = 1024 if seq % 1024 == 0 else 512
    per_batch = seq // tm
    return pl.pallas_call(
        _inproj_kernel,
        grid=(t // tm, PROJ_WIDTH // PROJ_N_TILE),
        in_specs=[
            pl.BlockSpec((tm, d), lambda i, n: (i, 0)),
            pl.BlockSpec((1, N_MOD, d), lambda i, n: (i // per_batch, 0, 0)),
            pl.BlockSpec((1, d), lambda i, n: (0, 0)),
            pl.BlockSpec((d, PROJ_N_TILE), lambda i, n: (0, n)),
        ],
        out_specs=pl.BlockSpec((tm, PROJ_N_TILE), lambda i, n: (i, n)),
        out_shape=jax.ShapeDtypeStruct((t, PROJ_WIDTH), BF16),
        scratch_shapes=[pltpu.VMEM((tm, d), BF16)],
        compiler_params=_params("arbitrary", "arbitrary"),
        name="in_proj",
    )(x2, mod, gain, w_main)


PARTS = 3
LR_STRIDE = 8
DT_ROWS = 2 * PARTS
X_BLOCKS = ("ev_fwd", "wv_fwd", "ev_bwd", "wv_bwd")
X_STRIDE = PARTS * SSM_HPG
LOG_DT_FLOOR = -200.0
LOG2_E = 1.4426950408889634
SSD_UNROLL = 2
DT_CHUNKS_PER_STEP = 2
GW_ALL = SSM_GROUPS * LANES


def _placement_constants():
    place = np.zeros((5 * PARTS, LANES, GW_ALL), np.float32)
    ones = np.zeros((2, 1, GW_ALL), np.float32)
    for g in range(SSM_GROUPS):
        for j in range(DT_COLS):
            src = g * DT_COLS + j
            base = g * LANES + j * LR_STRIDE
            for k in range(PARTS):
                place[k, src, base + k] = 1.0
                place[PARTS + k, src, base + PARTS + k] = -1.0
            place[2 * PARTS, src, base + DT_ROWS] = 1.0
            place[2 * PARTS + 1, src, base + DT_ROWS + 1] = 1.0
            ones[0, 0, base + PARTS:base + 2 * PARTS] = 1.0
            ones[1, 0, base:base + PARTS] = 1.0
        for r in range(SSM_HPG):
            fwd, bwd = g * DT_COLS + r, g * DT_COLS + SSM_HPG + r
            for k in range(PARTS):
                col = g * LANES + k * SSM_HPG + r
                place[3 * PARTS + k, fwd, col + 0 * X_STRIDE] = 1.0
                place[3 * PARTS + k, bwd, col + 2 * X_STRIDE] = 1.0
                place[4 * PARTS + k, fwd, col + 1 * X_STRIDE] = 1.0
                place[4 * PARTS + k, bwd, col + 3 * X_STRIDE] = 1.0
    return jnp.asarray(place, BF16), jnp.asarray(ones, F32)


def _expansion_constant():
    ex = np.zeros((LANES, len(X_BLOCKS) * GROUP_WIDTH), np.float32)
    for t in range(len(X_BLOCKS)):
        for k in range(PARTS):
            for r in range(SSM_HPG):
                row = t * X_STRIDE + k * SSM_HPG + r
                ex[row, t * GROUP_WIDTH + r * SSM_HEAD_DIM:t * GROUP_WIDTH + (r + 1) * SSM_HEAD_DIM] = 1.0
    return jnp.asarray(ex, BF16)


def _split3(x):
    hi = x.astype(BF16)
    r1 = x - hi.astype(F32)
    mid = r1.astype(BF16)
    lo = (r1 - mid.astype(F32)).astype(BF16)
    return hi, mid, lo


def _dt_kernel(x_ref, mod_ref, g_ref, wh_ref, wl_ref, bias_ref, alog_ref, place_ref, ones_ref,
               lmat_ref, rmat_ref, xmat_ref, dch_ref):
    q = SSM_CHUNK
    rows = DT_CHUNKS_PER_STEP * q
    h = _prenorm_modulate(x_ref[...], g_ref[...], mod_ref[0, 0:1, :], mod_ref[0, 1:2, :])
    hb = h.astype(BF16)
    h_lo = (h - hb.astype(F32)).astype(BF16)
    raw = (jnp.dot(hb, wh_ref[...], preferred_element_type=F32) + jnp.dot(hb, wl_ref[...], preferred_element_type=F32)
           + jnp.dot(h_lo, wh_ref[...], preferred_element_type=F32)) + bias_ref[...]
    dt = jnp.maximum(raw, 0.0) + jnp.log1p(jnp.exp(-jnp.abs(raw)))
    da = dt * (-jnp.exp(alog_ref[...]))
    row = lax.broadcasted_iota(jnp.int32, (rows, rows), 0)
    col = lax.broadcasted_iota(jnp.int32, (rows, rows), 1)
    tri = jnp.where((row >= col) & (row // q == col // q), 1.0, 0.0).astype(BF16)
    prefix = None
    for part in _split3(da):
        term = jnp.dot(tri, part, preferred_element_type=F32)
        prefix = term if prefix is None else prefix + term
    chunk_of_row = lax.broadcasted_iota(jnp.int32, (rows, 1), 0) // q
    tot = jnp.zeros((rows, LANES), F32)
    for c in range(DT_CHUNKS_PER_STEP):
        last = prefix[(c + 1) * q - 1:(c + 1) * q, :]
        tot = jnp.where(chunk_of_row == c, last, tot)
        dch_ref[0, c] = jnp.exp(last)
    lane = lax.broadcasted_iota(jnp.int32, (1, LANES), 1)
    is_fwd = (lane % DT_COLS) < SSM_HPG
    cum = jnp.where(is_fwd, prefix, tot - prefix + da)
    ev = jnp.exp(cum)
    wv = jnp.exp(tot - cum) * dt
    cum2 = cum * LOG2_E
    rsub = cum2 - jnp.maximum(jnp.log(dt), LOG_DT_FLOOR) * LOG2_E
    diag = jnp.maximum(jnp.log(dt + pltpu.roll(dt, SSM_HPG, 1)), LOG_DT_FLOOR) * LOG2_E

    def placed(first, value, n_parts=PARTS):
        acc = None
        for k, part in enumerate(_split3(value)[:n_parts]):
            term = jnp.dot(part, place_ref[first + k], preferred_element_type=F32)
            acc = term if acc is None else acc + term
        return acc

    lmat_ref[...] = (placed(0, cum2) + ones_ref[0]).astype(BF16)
    rmat_ref[...] = (placed(PARTS, rsub) + placed(2 * PARTS, diag, 2) + ones_ref[1]).T.astype(BF16)
    xmat_ref[...] = (placed(3 * PARTS, ev) + placed(4 * PARTS, wv)).astype(BF16)


def _dt_call(x2, mod, gain, w_dt, dt_bias, a_log, bsz, seq):
    t, d = x2.shape
    per_step = DT_CHUNKS_PER_STEP
    q = SSM_CHUNK * per_step
    nc = seq // q
    place, ones = _placement_constants()
    w_hi = w_dt.astype(BF16)
    return pl.pallas_call(
        _dt_kernel,
        grid=(bsz, nc),
        in_specs=[
            pl.BlockSpec((q, d), lambda b, c: (b * nc + c, 0)),
            pl.BlockSpec((1, N_MOD, d), lambda b, c: (b, 0, 0)),
            pl.BlockSpec((1, d), lambda b, c: (0, 0)),
            pl.BlockSpec((d, LANES), lambda b, c: (0, 0)),
            pl.BlockSpec((d, LANES), lambda b, c: (0, 0)),
            pl.BlockSpec((1, LANES), lambda b, c: (0, 0)),
            pl.BlockSpec((1, LANES), lambda b, c: (0, 0)),
            pl.BlockSpec(place.shape, lambda b, c: (0, 0, 0)),
            pl.BlockSpec(ones.shape, lambda b, c: (0, 0, 0)),
        ],
        out_specs=[
            pl.BlockSpec((q, GW_ALL), lambda b, c: (b * nc + c, 0)),
            pl.BlockSpec((GW_ALL, q), lambda b, c: (b, c)),
            pl.BlockSpec((q, GW_ALL), lambda b, c: (b * nc + c, 0)),
            pl.BlockSpec((1, per_step, 1, LANES), lambda b, c: (b, c, 0, 0)),
        ],
        out_shape=[
            jax.ShapeDtypeStruct((t, GW_ALL), BF16),
            jax.ShapeDtypeStruct((bsz * GW_ALL, seq), BF16),
            jax.ShapeDtypeStruct((t, GW_ALL), BF16),
            jax.ShapeDtypeStruct((bsz, nc * per_step, 1, LANES), F32),
        ],
        compiler_params=_params("arbitrary", "arbitrary"),
        name="dt_prep",
    )(x2, mod, gain, w_hi, (w_dt - w_hi.astype(F32)).astype(BF16), dt_bias, a_log, place, ones)


def _head_rms_rope(t, gain, cos, sin, lane):
    lo = lane < HEAD_DIM
    sq = t * t
    ss_lo = jnp.sum(jnp.where(lo, sq, 0.0), axis=-1, keepdims=True)
    ss_hi = jnp.sum(jnp.where(lo, 0.0, sq), axis=-1, keepdims=True)
    r = jnp.where(lo, lax.rsqrt(ss_lo * (1.0 / HEAD_DIM) + NORM_EPS), lax.rsqrt(ss_hi * (1.0 / HEAD_DIM) + NORM_EPS))
    tn = t * r * gain
    first_half = (lane % HEAD_DIM) < ROPE_HALF
    partner = jnp.where(first_half, pltpu.roll(tn, LANES - ROPE_HALF, 1), pltpu.roll(tn, ROPE_HALF, 1))
    return tn * cos + partner * sin


def _head_rms_rope_mxu(t, gain, cos, sin, head_sum, swap):
    ss = jnp.dot((t * t).astype(BF16), head_sum, preferred_element_type=F32)
    tn = t * lax.rsqrt(ss * (1.0 / HEAD_DIM) + NORM_EPS) * gain
    partner = jnp.dot(tn.astype(BF16), swap, preferred_element_type=F32)
    return tn * cos + partner * sin


def _window_masks(seq):
    blk = ATTN_BLOCK
    key = np.arange(3 * blk)[:, None]
    row = np.arange(blk)[None, :]
    in_band = (key - row >= 0) & (key - row <= 2 * WINDOW)
    nb = seq // blk
    masks = []
    for n in (0, 1, nb - 1):
        kidx = key + (n - 1) * blk
        masks.append(np.where(in_band & (kidx >= 0) & (kidx < seq), 0.0, NEG_BIG))
    return jnp.asarray(np.stack(masks), BF16)


def _sum_columns():
    ones = np.zeros((6 * ATTN_BLOCK, LANES), np.float32)
    ones[:3 * ATTN_BLOCK, 0] = 1.0
    ones[3 * ATTN_BLOCK:, 1] = 1.0
    return jnp.asarray(ones, BF16)


def _attn_kernel(sink_ref, q_ref, k_ref, v_ref, cos_ref, sin_ref, qg_ref, kg_ref, mask_ref, ones_ref, o_ref,
                 kpad, vpad, sc_scr, p_scr):
    blk = ATTN_BLOCK
    s = q_ref.shape[0]
    nb = s // blk
    lane = lax.broadcasted_iota(jnp.int32, (1, LANES), 1)
    lo = lane < HEAD_DIM

    zero_blk = jnp.zeros((blk, LANES), BF16)
    for j in range(2 * N_KV_HEADS):
        kpad[j, 0:blk, :] = zero_blk
        kpad[j, blk + s:2 * blk + s, :] = zero_blk
        vpad[j, 0:blk, :] = zero_blk
        vpad[j, blk + s:2 * blk + s, :] = zero_blk

    def prep(c, carry):
        r0 = pl.multiple_of(c * blk, blk)
        rows = pl.ds(r0, blk)
        dst = pl.ds(r0 + blk, blk)
        cos = cos_ref[0, rows, :]
        sin = sin_ref[0, rows, :]
        for tpair in range(N_KV_HEADS // 2):
            ls = slice(tpair * LANES, (tpair + 1) * LANES)
            kr = _head_rms_rope(k_ref[rows, ls].astype(F32), kg_ref[...], cos, sin, lane)
            vv = v_ref[rows, ls].astype(F32)
            for src, store in ((kr, kpad), (vv, vpad)):
                even_lo = jnp.where(lo, src, 0.0)
                odd_hi = jnp.where(lo, 0.0, src)
                g0, g1 = 2 * tpair, 2 * tpair + 1
                store[2 * g0, dst, :] = even_lo.astype(BF16)
                store[2 * g0 + 1, dst, :] = pltpu.roll(even_lo, HEAD_DIM, 1).astype(BF16)
                store[2 * g1, dst, :] = pltpu.roll(odd_hi, HEAD_DIM, 1).astype(BF16)
                store[2 * g1 + 1, dst, :] = odd_hi.astype(BF16)
        return carry

    lax.fori_loop(0, nb, prep, 0)

    ri = lax.broadcasted_iota(jnp.int32, (blk, blk), 0)
    ci = lax.broadcasted_iota(jnp.int32, (blk, blk), 1)
    eye = jnp.where(ri == ci, 1.0, 0.0).astype(BF16)
    head_sum = jnp.where(ri // HEAD_DIM == ci // HEAD_DIM, 1.0, 0.0).astype(BF16)
    cj = ci % HEAD_DIM
    swap = jnp.where(((cj < ROPE_HALF) & (ri == ci + ROPE_HALF)) | ((cj >= ROPE_HALF) & (cj < ROPE_DIMS)
                                                                      & (ri == ci - ROPE_HALF)), 1.0, 0.0).astype(BF16)

    def qblock(n, carry):
        r0 = pl.multiple_of(n * blk, blk)
        rows = pl.ds(r0, blk)
        band = pl.ds(r0, 3 * blk)
        cos = cos_ref[0, rows, :]
        sin = sin_ref[0, rows, :]
        edge = jnp.where(n == 0, 0, jnp.where(n == nb - 1, 2, 1))
        mask_t = mask_ref[edge]
        n_pairs = N_Q_HEADS // 2
        qts = [q_ref[rows, pair * LANES:(pair + 1) * LANES].astype(F32) for pair in range(n_pairs)]
        sss = [jnp.dot((qt * qt).astype(BF16), head_sum, preferred_element_type=F32) for qt in qts]
        tns = [qt * lax.rsqrt(ss * (1.0 / HEAD_DIM) + NORM_EPS) * qg_ref[...] for qt, ss in zip(qts, sss)]
        partners = [jnp.dot(tn.astype(BF16), swap, preferred_element_type=F32) for tn in tns]
        for pair in range(n_pairs):
            g = pair // 2
            qq = (tns[pair] * cos + partners[pair] * sin).astype(BF16)
            lhs = jnp.concatenate([qq, eye], axis=1)
            keys = jnp.concatenate([jnp.concatenate([kpad[2 * g, band, :], mask_t], axis=1),
                                    jnp.concatenate([kpad[2 * g + 1, band, :], mask_t], axis=1)], axis=0)
            sc_scr[pair] = lax.dot_general(lhs, keys, (((1,), (1,)), ((), ())), preferred_element_type=F32)
        corr = []
        for pair in range(n_pairs):
            ms = []
            for par in range(2):
                cols = slice(par * 3 * blk, (par + 1) * 3 * blk)
                sh = sc_scr[pair, :, cols]
                sink = sink_ref[2 * pair + par]
                m = jnp.maximum(jnp.max(sh, axis=-1, keepdims=True), sink)
                p_scr[pair, :, cols] = jnp.exp2(sh - m).astype(BF16)
                ms.append(jnp.exp2(sink - m))
            corr.append(ms)
        for pair in range(n_pairs):
            g = pair // 2
            vals = jnp.concatenate([vpad[2 * g, band, :], vpad[2 * g + 1, band, :]], axis=0)
            out = jnp.dot(p_scr[pair], jnp.concatenate([vals, ones_ref[...]], axis=1), preferred_element_type=F32)
            inv_lo = 1.0 / (out[:, LANES:LANES + 1] + corr[pair][0])
            inv_hi = 1.0 / (out[:, LANES + 1:LANES + 2] + corr[pair][1])
            o_ref[rows, pair * LANES:(pair + 1) * LANES] = (out[:, 0:LANES] * jnp.where(lo, inv_lo, inv_hi)).astype(BF16)
        return carry

    lax.fori_loop(0, nb, qblock, 0)


def _attn_call(proj, cos_t, sin_t, q_gain, k_gain, sink, bsz, seq):
    t = proj.shape[0]
    masks = _window_masks(seq)
    ones = _sum_columns()
    return pl.pallas_call(
        _attn_kernel,
        grid_spec=pltpu.PrefetchScalarGridSpec(
            num_scalar_prefetch=1,
            grid=(bsz,),
            in_specs=[
                pl.BlockSpec((seq, ATTN_WIDTH), lambda b, sk: (b, COL_Q // ATTN_WIDTH)),
                pl.BlockSpec((seq, KV_WIDTH), lambda b, sk: (b, COL_K // KV_WIDTH)),
                pl.BlockSpec((seq, KV_WIDTH), lambda b, sk: (b, COL_V // KV_WIDTH)),
                pl.BlockSpec((1, seq, LANES), lambda b, sk: (b, 0, 0)),
                pl.BlockSpec((1, seq, LANES), lambda b, sk: (b, 0, 0)),
                pl.BlockSpec((1, LANES), lambda b, sk: (0, 0)),
                pl.BlockSpec((1, LANES), lambda b, sk: (0, 0)),
                pl.BlockSpec(masks.shape, lambda b, sk: (0, 0, 0)),
                pl.BlockSpec(ones.shape, lambda b, sk: (0, 0)),
            ],
            out_specs=pl.BlockSpec((seq, ATTN_WIDTH), lambda b, sk: (b, 0)),
            scratch_shapes=[
                pltpu.VMEM((2 * N_KV_HEADS, seq + 2 * ATTN_BLOCK, LANES), BF16),
                pltpu.VMEM((2 * N_KV_HEADS, seq + 2 * ATTN_BLOCK, LANES), BF16),
                pltpu.VMEM((N_Q_HEADS // 2, ATTN_BLOCK, 6 * ATTN_BLOCK), F32),
                pltpu.VMEM((N_Q_HEADS // 2, ATTN_BLOCK, 6 * ATTN_BLOCK), BF16),
            ],
        ),
        out_shape=jax.ShapeDtypeStruct((t, ATTN_WIDTH), BF16),
        compiler_params=_params("arbitrary"),
        name="window_attn",
    )(sink, proj, proj, proj, cos_t, sin_t, q_gain, k_gain, masks, ones)


def _ssd_kernel(dch_ref, xs_ref, b_ref, c_ref, z_ref, cwx_ref, cwb_ref, cwc_ref, cbx_ref, cbb_ref, cbc_ref,
                lmat_ref, rmat_ref, xmat_ref, ex_ref, dskip_ref, ng_ref, o_ref,
                upad, xs_c, b_c, c_c, y_acc, st, st_b, d_all, m_all):
    q = SSM_CHUNK
    s = xs_ref.shape[0]
    nc = s // q
    pad = 16
    halo = SSM_CONV // 2
    win = q + pad
    width = GROUP_WIDTH + 2 * SSM_STATE

    upad[0:pad, :] = jnp.zeros((pad, width), F32)
    upad[pad + s:2 * pad + s, :] = jnp.zeros((pad, width), F32)

    def fill(c, carry):
        r0 = pl.multiple_of(c * q, q)
        rows = pl.ds(r0, q)
        dst = pl.ds(r0 + pad, q)
        upad[dst, 0:GROUP_WIDTH] = xs_ref[rows, :].astype(F32)
        upad[dst, GROUP_WIDTH:GROUP_WIDTH + SSM_STATE] = b_ref[rows, :].astype(F32)
        upad[dst, GROUP_WIDTH + SSM_STATE:width] = c_ref[rows, :].astype(F32)
        return carry

    lax.fori_loop(0, nc, fill, 0)

    def conv_cols(r0, lo_col, n_col, w_ref, bias_ref, dst_ref):
        window = upad[pl.ds(r0 + pad - SUBLANES, win), lo_col:lo_col + n_col]
        acc = jnp.zeros((q, n_col), F32) + bias_ref[...]
        for k in range(SSM_CONV):
            shift = SUBLANES - halo + k
            acc = acc + pltpu.roll(window, win - shift, 0)[0:q] * w_ref[k:k + 1, :]
        dst_ref[pl.ds(r0, q), :] = acc * (1.0 / (1.0 + jnp.exp(-acc)))

    def conv(c, carry):
        r0 = pl.multiple_of(c * q, q)
        for j in range(GROUP_WIDTH // LANES):
            conv_cols(r0, j * LANES, LANES, cwx_ref.at[:, j * LANES:(j + 1) * LANES],
                      cbx_ref.at[:, j * LANES:(j + 1) * LANES], xs_c.at[:, j * LANES:(j + 1) * LANES])
        conv_cols(r0, GROUP_WIDTH, SSM_STATE, cwb_ref, cbb_ref, b_c)
        conv_cols(r0, GROUP_WIDTH + SSM_STATE, SSM_STATE, cwc_ref, cbc_ref, c_c)
        return carry

    lax.fori_loop(0, nc, conv, 0)

    row = lax.broadcasted_iota(jnp.int32, (q, q), 0)
    col = lax.broadcasted_iota(jnp.int32, (q, q), 1)
    below = row > col
    on_diag = row == col
    head_of_row = row // LR_STRIDE
    low_half = lax.broadcasted_iota(jnp.int32, (q, LANES), 1) < SSM_HEAD_DIM
    gw = GROUP_WIDTH

    def expand(xm, block):
        return jnp.dot(xm, ex_ref[:, block * gw:(block + 1) * gw], preferred_element_type=F32)

    def decay_rows(c, first):
        return jnp.concatenate(
            [jnp.full((1, SSM_HEAD_DIM), dch_ref[0, 0, c, first + h], F32) for h in range(SSM_HPG)], axis=-1)

    def rows_of(c):
        return pl.ds(pl.multiple_of(c * q, q), q)

    def pair_step(i, j, slot):
        d_scr, m_scr = d_all.at[slot], m_all.at[slot]
        ri, rj = rows_of(i), rows_of(j)
        xs_i, xs_j = xs_c[ri, :], xs_c[rj, :]
        b_i, b_j = b_c[ri, :].astype(BF16), b_c[rj, :].astype(BF16)
        c_i, c_j = c_c[ri, :].astype(BF16), c_c[rj, :].astype(BF16)
        lm, rm = lmat_ref[ri, :], rmat_ref[:, ri]
        xm_i, xm_j = xmat_ref[ri, :], xmat_ref[rj, :]
        state_f, state_b = st[...], st_b[...]
        zero_b = jnp.zeros((q, LANES), BF16)

        wv_f, wv_b = expand(xm_i, 1), expand(xm_j, 3)
        ev_f, ev_b = expand(xm_i, 0), expand(xm_j, 2)
        off_f = jnp.dot(c_i, state_f.astype(BF16), preferred_element_type=F32)
        off_b = jnp.dot(c_j, state_b.astype(BF16), preferred_element_type=F32)
        cb = lax.dot_general(c_i, b_i, (((1,), (1,)), ((), ())), preferred_element_type=F32)
        for h in range(SSM_HPG):
            rhs = jnp.concatenate([jnp.where(head_of_row == h, rm, zero_b),
                                   jnp.where(head_of_row == SSM_HPG + h, rm, zero_b)], axis=1)
            d_scr[:, 2 * h * q:2 * (h + 1) * q] = jnp.dot(lm, rhs, preferred_element_type=F32)

        upd_f = lax.dot_general(b_i, (xs_i * wv_f).astype(BF16), (((0,), (0,)), ((), ())), preferred_element_type=F32)
        upd_b = lax.dot_general(b_j, (xs_j * wv_b).astype(BF16), (((0,), (0,)), ((), ())), preferred_element_type=F32)
        st[...] = state_f * decay_rows(i, 0) + upd_f
        st_b[...] = state_b * decay_rows(j, SSM_HPG) + upd_b
        y_i = off_f * ev_f + xs_i * dskip_ref[...]
        y_j = off_b * ev_b

        for h in range(SSM_HPG):
            d_f = d_scr[:, 2 * h * q:(2 * h + 1) * q]
            d_b = d_scr[:, (2 * h + 1) * q:(2 * h + 2) * q]
            diag_row = (SSM_HPG + h) * LR_STRIDE + DT_ROWS
            diag = rm[diag_row:diag_row + 1, :].astype(F32) + rm[diag_row + 1:diag_row + 2, :].astype(F32)
            e = jnp.exp2(jnp.where(below, d_f, jnp.where(on_diag, diag, d_b)))
            m_scr[:, h * q:(h + 1) * q] = (cb * e).astype(BF16)
        xs_b16 = xs_i.astype(BF16)
        ys = []
        for pair in range(SSM_HPG // 2):
            xpair = xs_b16[:, pair * LANES:(pair + 1) * LANES]
            stacked = jnp.concatenate([jnp.where(low_half, xpair, zero_b), jnp.where(low_half, zero_b, xpair)], axis=0)
            ys.append(jnp.dot(m_scr[:, 2 * pair * q:2 * (pair + 1) * q], stacked, preferred_element_type=F32))
        return y_i + jnp.concatenate(ys, axis=-1), y_j

    def finish(c, y):
        rows = rows_of(c)
        zz = z_ref[rows, :].astype(F32)
        y = y * (zz * (1.0 / (1.0 + jnp.exp(-zz))))
        ms = jnp.mean(y * y, axis=-1, keepdims=True)
        o_ref[rows, :] = (y * lax.rsqrt(ms + NORM_EPS) * ng_ref[...]).astype(BF16)

    st[...] = jnp.zeros_like(st)
    st_b[...] = jnp.zeros_like(st_b)

    unroll = SSD_UNROLL

    def outward(t, carry):
        for u in range(unroll):
            i = t * unroll + u
            j = nc - 1 - i
            y_i, y_j = pair_step(i, j, u)
            y_acc[rows_of(i), :] = y_i
            y_acc[rows_of(j), :] = y_j
        return carry

    def inward(t, carry):
        for u in range(unroll):
            i = t * unroll + u
            j = nc - 1 - i
            y_i, y_j = pair_step(i, j, u)
            finish(i, y_acc[rows_of(i), :] + y_i)
            finish(j, y_acc[rows_of(j), :] + y_j)
        return carry

    lax.fori_loop(0, nc // (2 * unroll), outward, 0)
    lax.fori_loop(nc // (2 * unroll), nc // unroll, inward, 0)


def _ssd_call(proj, conv_w, conv_b, lmat, rmat, xmat, dch, d_skip, norm_g, bsz, seq):
    t = proj.shape[0]
    g_n = SSM_GROUPS
    q = SSM_CHUNK
    gw = GROUP_WIDTH
    width = GROUP_WIDTH + 2 * SSM_STATE
    ex = _expansion_constant()
    return pl.pallas_call(
        _ssd_kernel,
        grid=(bsz, g_n),
        in_specs=[
            pl.BlockSpec((1, 1, seq // q, DT_COLS), lambda b, g: (b, g, 0, 0), memory_space=pltpu.SMEM),
            pl.BlockSpec((seq, gw), lambda b, g: (b, COL_XS // gw + g)),
            pl.BlockSpec((seq, SSM_STATE), lambda b, g: (b, COL_B // SSM_STATE + g)),
            pl.BlockSpec((seq, SSM_STATE), lambda b, g: (b, COL_C // SSM_STATE + g)),
            pl.BlockSpec((seq, gw), lambda b, g: (b, COL_Z // gw + g)),
            pl.BlockSpec((SSM_CONV, gw), lambda b, g: (0, g)),
            pl.BlockSpec((SSM_CONV, SSM_STATE), lambda b, g: (0, SSM_INNER // SSM_STATE + g)),
            pl.BlockSpec((SSM_CONV, SSM_STATE), lambda b, g: (0, SSM_INNER // SSM_STATE + g_n + g)),
            pl.BlockSpec((1, gw), lambda b, g: (0, g)),
            pl.BlockSpec((1, SSM_STATE), lambda b, g: (0, SSM_INNER // SSM_STATE + g)),
            pl.BlockSpec((1, SSM_STATE), lambda b, g: (0, SSM_INNER // SSM_STATE + g_n + g)),
            pl.BlockSpec((seq, LANES), lambda b, g: (b, g)),
            pl.BlockSpec((LANES, seq), lambda b, g: (b * g_n + g, 0)),
            pl.BlockSpec((seq, LANES), lambda b, g: (b, g)),
            pl.BlockSpec(ex.shape, lambda b, g: (0, 0)),
            pl.BlockSpec((1, gw), lambda b, g: (0, g)),
            pl.BlockSpec((1, gw), lambda b, g: (0, g)),
        ],
        out_specs=pl.BlockSpec((seq, gw), lambda b, g: (b, g)),
        out_shape=jax.ShapeDtypeStruct((t, SSM_INNER), BF16),
        scratch_shapes=[
            pltpu.VMEM((seq + 32, width), F32),
            pltpu.VMEM((seq, gw), F32),
            pltpu.VMEM((seq, SSM_STATE), F32),
            pltpu.VMEM((seq, SSM_STATE), F32),
            pltpu.VMEM((seq, gw), F32),
            pltpu.VMEM((SSM_STATE, gw), F32),
            pltpu.VMEM((SSM_STATE, gw), F32),
            pltpu.VMEM((SSD_UNROLL, q, 2 * SSM_HPG * q), F32),
            pltpu.VMEM((SSD_UNROLL, q, SSM_HPG * q), BF16),
        ],
        compiler_params=_params("arbitrary", "arbitrary"),
        name="ssd_mixer",
    )(dch, proj, proj, proj, proj, conv_w, conv_w, conv_w, conv_b, conv_b, conv_b,
      lmat, rmat, xmat, ex, d_skip, norm_g)


def _outproj_kernel(attn_ref, ssm_ref, ga_ref, gs_ref, x_ref, mod_ref, wa_ref, ws_ref, wo_ref, o_ref):
    ya = jnp.dot(attn_ref[...], wa_ref[...], preferred_element_type=F32)
    ys = jnp.dot(ssm_ref[...], ws_ref[...], preferred_element_type=F32)
    ga = ga_ref[...].astype(F32)
    gs = gs_ref[...].astype(F32)
    merged = ya * (1.0 / (1.0 + jnp.exp(-ga))) + ys * (1.0 / (1.0 + jnp.exp(-gs)))
    y = jnp.dot(merged.astype(BF16), wo_ref[...], preferred_element_type=F32)
    o_ref[...] = x_ref[...] + mod_ref[0, 2:3, :] * y


def _outproj_call(attn, ssm, proj, x2, mod, w_attn_o, w_ssm_o, w_out, seq):
    t, d = x2.shape
    tm = 512
    per_batch = seq // tm
    const = lambda i: (0, 0)
    return pl.pallas_call(
        _outproj_kernel,
        grid=(t // tm,),
        in_specs=[
            pl.BlockSpec((tm, ATTN_WIDTH), lambda i: (i, 0)),
            pl.BlockSpec((tm, SSM_INNER), lambda i: (i, 0)),
            pl.BlockSpec((tm, d), lambda i: (i, COL_GATE_ATTN // d)),
            pl.BlockSpec((tm, d), lambda i: (i, COL_GATE_SSM // d)),
            pl.BlockSpec((tm, d), lambda i: (i, 0)),
            pl.BlockSpec((1, N_MOD, d), lambda i: (i // per_batch, 0, 0)),
            pl.BlockSpec((ATTN_WIDTH, d), const),
            pl.BlockSpec((SSM_INNER, d), const),
            pl.BlockSpec((d, d), const),
        ],
        out_specs=pl.BlockSpec((tm, d), lambda i: (i, 0)),
        out_shape=jax.ShapeDtypeStruct((t, d), F32),
        compiler_params=_params("arbitrary"),
        name="out_proj",
    )(attn, ssm, proj, proj, x2, mod, w_attn_o, w_ssm_o, w_out)


def _permute_in_proj(w_in):
    w_main = jnp.concatenate(
        [w_in[:, _O_Q:_O_K], w_in[:, _O_GATES:], w_in[:, _O_Z:_O_XBC], w_in[:, _O_XBC:_O_XBC + SSM_INNER],
         w_in[:, _O_K:_O_Z], w_in[:, _O_XBC + SSM_INNER:_O_DT]], axis=1).astype(BF16)
    return w_main


def _dt_permutation():
    idx = np.zeros(2 * SSM_HEADS, np.int32)
    for g in range(SSM_GROUPS):
        for d in range(2):
            for r in range(SSM_HPG):
                idx[g * DT_COLS + d * SSM_HPG + r] = d * SSM_HEADS + g * SSM_HPG + r
    return idx


def _pad_lanes(a):
    return jnp.pad(a, [(0, 0)] * (a.ndim - 1) + [(0, LANES - a.shape[-1])])


def _mixer_layer(x2, mod, cos_t, sin_t, bsz, seq, norm_g, w_in, q_norm_g, k_norm_g, attn_sink, conv_w, conv_b,
                 a_log, dt_bias, ssm_d, ssm_norm_g, w_attn_o, w_ssm_o, w_out):
    d = x2.shape[1]
    gain = norm_g.reshape(1, d)
    perm = _dt_permutation()
    w_dt = _pad_lanes(w_in[:, _O_DT:_O_GATES][:, perm])
    dt_b = _pad_lanes(dt_bias.reshape(-1)[perm].reshape(1, -1))
    a_lg = _pad_lanes(a_log.reshape(-1)[perm].reshape(1, -1))

    proj = _inproj_call(x2, mod, gain, _permute_in_proj(w_in), seq)
    lmat, rmat, xmat, dch = _dt_call(x2, mod, gain, w_dt, dt_b, a_lg, bsz, seq)
    nc = seq // SSM_CHUNK
    dch = dch[:, :, 0, :2 * SSM_HEADS].reshape(bsz, nc, SSM_GROUPS, DT_COLS).transpose(0, 2, 1, 3)

    attn = _attn_call(proj, cos_t, sin_t, jnp.tile(q_norm_g, 2).reshape(1, LANES) * (HEAD_DIM ** -0.5 * LOG2_E),
                      jnp.tile(k_norm_g, 2).reshape(1, LANES), attn_sink * LOG2_E, bsz, seq)
    ssm = _ssd_call(proj, conv_w, conv_b.reshape(1, -1), lmat, rmat, xmat, dch,
                    jnp.repeat(ssm_d, SSM_HEAD_DIM).reshape(1, SSM_INNER), ssm_norm_g.reshape(1, SSM_INNER),
                    bsz, seq)
    return _outproj_call(attn, ssm, proj, x2, mod, w_attn_o.astype(BF16), w_ssm_o.astype(BF16),
                         w_out.astype(BF16), seq)


ROUTE_BLOCK = 256
SLAB = SUBLANES
LOCAL_ROWS = ROUTE_BLOCK * TOP_K + ROUTE_BLOCK
LOCAL_SLABS = LOCAL_ROWS // SLAB
TILE_ROWS = 512
TILE_SLABS = TILE_ROWS // SLAB


def _route_kernel(x_ref, mod_ref, g_ref, rwh_ref, rwl_ref, rb_ref, xloc_ref, tokrow_ref, colw_ref, cnt_ref):
    tb = ROUTE_BLOCK
    h = _prenorm_modulate(x_ref[...], g_ref[...], mod_ref[0, 3:4, :], mod_ref[0, 4:5, :])
    hb = h.astype(BF16)
    h_lo = (h - hb.astype(F32)).astype(BF16)
    logits = (jnp.dot(hb, rwh_ref[...], preferred_element_type=F32)
              + jnp.dot(hb, rwl_ref[...], preferred_element_type=F32)
              + jnp.dot(h_lo, rwh_ref[...], preferred_element_type=F32)) + rb_ref[...]
    v = logits.T[0:N_EXPERTS, :]
    erow = lax.broadcasted_iota(jnp.int32, (N_EXPERTS, tb), 0)
    hots, tops = [], []
    for _ in range(TOP_K):
        mk = jnp.max(v, axis=0, keepdims=True)
        first = jnp.min(jnp.where(v == mk, erow, N_EXPERTS), axis=0, keepdims=True)
        hot = erow == first
        v = jnp.where(hot, -jnp.inf, v)
        hots.append(hot)
        tops.append(mk)
    ps = [jnp.exp(mk - tops[0]) for mk in tops]
    denom = ps[0] + ps[1] + ps[2] + ps[3]
    sel = jnp.zeros((N_EXPERTS, tb), F32)
    for hot in hots:
        sel = sel + jnp.where(hot, 1.0, 0.0)
    ti = lax.broadcasted_iota(jnp.int32, (tb, tb), 0)
    tj = lax.broadcasted_iota(jnp.int32, (tb, tb), 1)
    before = jnp.where(ti < tj, 1.0, 0.0).astype(BF16)
    rank = jnp.dot(sel.astype(BF16), before, preferred_element_type=F32)
    count = jnp.sum(sel, axis=1, keepdims=True)
    slabs = jnp.floor((count + (SLAB - 1)) * (1.0 / SLAB))
    ei = lax.broadcasted_iota(jnp.int32, (N_EXPERTS, N_EXPERTS), 0)
    ej = lax.broadcasted_iota(jnp.int32, (N_EXPERTS, N_EXPERTS), 1)
    earlier = jnp.where(ej < ei, 1.0, 0.0).astype(BF16)
    slabs_b = jnp.broadcast_to(slabs, (N_EXPERTS, tb))
    start = jnp.dot(earlier, slabs_b.astype(BF16), preferred_element_type=F32)
    dest = start * SLAB + rank
    dki = [jnp.sum(jnp.where(hot, dest, 0.0), axis=0, keepdims=True).astype(jnp.int32) for hot in hots]
    wks = [p / denom for p in ps]
    cnt_ref[0] = slabs_b[:, 0:LANES]
    tok1 = (lax.broadcasted_iota(jnp.int32, (SUBLANES, tb), 1) + 1).astype(BF16)
    for c in range(LOCAL_ROWS // tb):
        rs = slice(c * tb, (c + 1) * tb)
        ri = lax.broadcasted_iota(jnp.int32, (tb, tb), 0) + c * tb
        weighted = jnp.where(ri == dki[0], wks[0], jnp.where(ri == dki[1], wks[1], jnp.where(
            ri == dki[2], wks[2], jnp.where(ri == dki[3], wks[3], 0.0))))
        onehot = jnp.where(weighted > 0.0, 1.0, 0.0).astype(BF16)
        xloc_ref[0, rs, :] = jnp.dot(onehot, hb, preferred_element_type=F32)
        colw_ref[0, rs, :] = jnp.broadcast_to(jnp.sum(weighted, axis=1, keepdims=True), (tb, LANES))
        tokrow_ref[0, :, rs] = lax.dot_general(tok1, onehot, (((1,), (1,)), ((), ())), preferred_element_type=F32)


def _route_call(x2, mod, gain, router_w, router_b, seq):
    t, d = x2.shape
    tb = ROUTE_BLOCK
    nblk = t // tb
    per_batch = seq // tb
    rw = _pad_lanes(router_w)
    rw_hi = rw.astype(BF16)
    return pl.pallas_call(
        _route_kernel,
        grid=(nblk,),
        in_specs=[
            pl.BlockSpec((tb, d), lambda i: (i, 0)),
            pl.BlockSpec((1, N_MOD, d), lambda i: (i // per_batch, 0, 0)),
            pl.BlockSpec((1, d), lambda i: (0, 0)),
            pl.BlockSpec((d, LANES), lambda i: (0, 0)),
            pl.BlockSpec((d, LANES), lambda i: (0, 0)),
            pl.BlockSpec((1, LANES), lambda i: (0, 0)),
        ],
        out_specs=[
            pl.BlockSpec((1, LOCAL_ROWS, d), lambda i: (i, 0, 0)),
            pl.BlockSpec((1, SUBLANES, LOCAL_ROWS), lambda i: (i, 0, 0)),
            pl.BlockSpec((1, LOCAL_ROWS, LANES), lambda i: (i, 0, 0)),
            pl.BlockSpec((1, N_EXPERTS, LANES), lambda i: (i, 0, 0)),
        ],
        out_shape=[
            jax.ShapeDtypeStruct((nblk, LOCAL_ROWS, d), F32),
            jax.ShapeDtypeStruct((nblk, SUBLANES, LOCAL_ROWS), F32),
            jax.ShapeDtypeStruct((nblk, LOCAL_ROWS, LANES), F32),
            jax.ShapeDtypeStruct((nblk, N_EXPERTS, LANES), F32),
        ],
        compiler_params=_params("arbitrary"),
        name="moe_route",
    )(x2, mod, gain, rw_hi, (rw - rw_hi.astype(F32)).astype(BF16), _pad_lanes(router_b.reshape(1, -1)))


def _slab_plan(slab_counts):
    nblk = slab_counts.shape[0]
    max_slabs = nblk * LOCAL_SLABS + N_EXPERTS * (TILE_SLABS - 1)
    max_tiles = -(-max_slabs // TILE_SLABS)
    c8 = slab_counts.astype(jnp.int32)
    local_start = jnp.cumsum(c8, axis=1) - c8
    per_expert = jnp.sum(c8, axis=0)
    tiles_e = (per_expert + TILE_SLABS - 1) // TILE_SLABS
    tile_start = jnp.cumsum(tiles_e) - tiles_e
    num_tiles = jnp.sum(tiles_e)
    expert_start = tile_start * TILE_SLABS
    block_off = jnp.cumsum(c8, axis=0) - c8
    seg_start = expert_start[None, :] + block_off

    sl = jnp.arange(LOCAL_SLABS, dtype=jnp.int32)[None, :, None]
    in_seg = (local_start[:, None, :] <= sl) & (sl < (local_start + c8)[:, None, :])
    slab_pos = jnp.sum(jnp.where(in_seg, (seg_start - local_start)[:, None, :] + sl, 0), axis=-1).reshape(-1)

    p = jnp.arange(max_tiles * TILE_SLABS, dtype=jnp.int32)[:, None]
    in_exp = (expert_start[None, :] <= p) & (p < (expert_start + per_expert)[None, :])
    off = p - jnp.sum(jnp.where(in_exp, expert_start[None, :], 0), axis=-1, keepdims=True)
    table = jnp.concatenate([block_off.T, c8.T, local_start.T], axis=1).astype(F32)
    picked = jnp.round(jnp.dot(in_exp.astype(F32), table, precision=HIGHEST)).astype(jnp.int32)
    boff, cnt, lst = picked[:, :nblk], picked[:, nblk:2 * nblk], picked[:, 2 * nblk:]
    in_blk = (boff <= off) & (off < boff + cnt)
    blk_base = jnp.arange(nblk, dtype=jnp.int32)[None, :] * LOCAL_SLABS
    slab_src = jnp.sum(jnp.where(in_blk, blk_base + lst - boff + off, 0), axis=-1)

    ti = jnp.arange(max_tiles, dtype=jnp.int32)
    tile_expert = jnp.clip(jnp.sum(tile_start[None, :] <= jnp.minimum(ti, num_tiles - 1)[:, None], axis=-1) - 1,
                           0, N_EXPERTS - 1)
    return tile_expert.astype(jnp.int32), slab_src.astype(jnp.int32), slab_pos.astype(jnp.int32), \
        num_tiles.reshape(1).astype(jnp.int32), max_tiles


def _expert_kernel(te_ref, src_ref, nt_ref, xloc_hbm, wg_ref, bg_ref, wu_ref, bu_ref, wd_ref, bd_ref, y_ref,
                   xbuf, sem, wg_s, wu_s, wd_s):
    i = pl.program_id(0)
    nt = nt_ref[0]

    def slab_copy(tile, slot, j):
        return pltpu.make_async_copy(xloc_hbm.at[src_ref[tile * TILE_SLABS + j]], xbuf.at[slot, j], sem.at[slot])

    def issue(tile, slot):
        for j in range(TILE_SLABS):
            slab_copy(tile, slot, j).start()

    @pl.when(i == 0)
    def _():
        issue(0, 0)

    @pl.when(i + 1 < nt)
    def _():
        issue(i + 1, (i + 1) % 2)

    @pl.when(i < nt)
    def _():
        slot = i % 2
        for j in range(TILE_SLABS):
            slab_copy(i, slot, j).wait()
        new_expert = jnp.logical_or(i == 0, te_ref[i] != te_ref[jnp.maximum(i - 1, 0)])

        @pl.when(new_expert)
        def _():
            rows = 128
            for src, dst in ((wg_ref, wg_s), (wu_ref, wu_s), (wd_ref, wd_s)):
                def cast(c, carry, src=src, dst=dst):
                    r = pl.ds(pl.multiple_of(c * rows, rows), rows)
                    dst[r, :] = src[0, 0, r, :].astype(BF16)
                    return carry
                lax.fori_loop(0, src.shape[2] // rows, cast, 0)

        x = xbuf[slot].reshape(TILE_ROWS, xbuf.shape[-1]).astype(BF16)
        gate = jnp.dot(x, wg_s[...], preferred_element_type=F32) + bg_ref[0, 0]
        up = jnp.dot(x, wu_s[...], preferred_element_type=F32) + bu_ref[0, 0]
        glu = jnp.minimum(gate, SWIGLU_LIMIT)
        lin = jnp.clip(up, -SWIGLU_LIMIT, SWIGLU_LIMIT)
        act = glu * (1.0 / (1.0 + jnp.exp(-SWIGLU_ALPHA * glu))) * (lin + 1.0)
        y_ref[...] = jnp.dot(act.astype(BF16), wd_s[...], preferred_element_type=F32) + bd_ref[0, 0]

    @pl.when(i >= nt)
    def _():
        y_ref[...] = jnp.zeros_like(y_ref)


def _expert_call(tile_expert, slab_src, num_tiles, max_tiles, xloc, layer, w_gate, b_gate, w_up, b_up, w_down, b_down):
    nblk, _, d = xloc.shape
    depth, _, _, ff = w_gate.shape
    xloc3 = xloc.reshape(nblk * LOCAL_SLABS, SLAB, d)
    wspec = lambda k, n: pl.BlockSpec((1, 1, k, n), lambda i, te, ss, nt: (layer, te[i], 0, 0))
    return pl.pallas_call(
        _expert_kernel,
        grid_spec=pltpu.PrefetchScalarGridSpec(
            num_scalar_prefetch=3,
            grid=(max_tiles,),
            in_specs=[
                pl.BlockSpec(memory_space=pl.ANY),
                wspec(d, ff), wspec(1, ff), wspec(d, ff), wspec(1, ff), wspec(ff, d), wspec(1, d),
            ],
            out_specs=pl.BlockSpec((TILE_ROWS, d), lambda i, te, ss, nt: (i, 0)),
            scratch_shapes=[
                pltpu.VMEM((2, TILE_SLABS, SLAB, d), F32),
                pltpu.SemaphoreType.DMA((2,)),
                pltpu.VMEM((d, ff), BF16),
                pltpu.VMEM((d, ff), BF16),
                pltpu.VMEM((ff, d), BF16),
            ],
        ),
        out_shape=jax.ShapeDtypeStruct((max_tiles * TILE_ROWS, d), F32),
        compiler_params=_params("arbitrary"),
        name="moe_experts",
    )(tile_expert, slab_src, num_tiles, xloc3, w_gate, b_gate.reshape(depth, N_EXPERTS, 1, ff), w_up,
      b_up.reshape(depth, N_EXPERTS, 1, ff), w_down, b_down.reshape(depth, N_EXPERTS, 1, d))


COMBINE_ISSUE_UNROLL = 8


def _combine_kernel(pos_ref, y_hbm, tokrow_ref, colw_ref, x_ref, mod_ref, o_ref, ybuf, sem):
    b = pl.program_id(0)
    nb = pl.num_programs(0)
    tb = ROUTE_BLOCK
    d = x_ref.shape[-1]

    def slab_copy(blk, slot, j):
        return pltpu.make_async_copy(y_hbm.at[pos_ref[blk * LOCAL_SLABS + j]], ybuf.at[slot, j], sem.at[slot])

    def issue(blk, slot):
        def body(jj, carry):
            for u in range(COMBINE_ISSUE_UNROLL):
                slab_copy(blk, slot, jj * COMBINE_ISSUE_UNROLL + u).start()
            return carry
        lax.fori_loop(0, LOCAL_SLABS // COMBINE_ISSUE_UNROLL, body, 0)

    @pl.when(b == 0)
    def _():
        issue(0, 0)

    @pl.when(b + 1 < nb)
    def _():
        issue(b + 1, (b + 1) % 2)

    slot = b % 2
    pltpu.make_async_copy(y_hbm.at[pl.ds(0, LOCAL_SLABS)], ybuf.at[slot], sem.at[slot]).wait()

    tok1 = (lax.broadcasted_iota(jnp.int32, (tb, tb), 0) + 1).astype(F32)
    acc = jnp.zeros((tb, d), F32)
    slabs_per_chunk = tb // SLAB
    for c in range(LOCAL_ROWS // tb):
        rs = slice(c * tb, (c + 1) * tb)
        y = ybuf[slot, c * slabs_per_chunk:(c + 1) * slabs_per_chunk].reshape(tb, d)
        w = colw_ref[0, rs, :]
        yw = jnp.concatenate([y[:, t * LANES:(t + 1) * LANES] * w for t in range(d // LANES)], axis=1).astype(BF16)
        unperm = jnp.where(tok1 == tokrow_ref[0, 0:1, rs], 1.0, 0.0).astype(BF16)
        acc = acc + jnp.dot(unperm, yw, preferred_element_type=F32)
    o_ref[...] = x_ref[...] + mod_ref[0, 5:6, :] * acc


def _combine_call(slab_pos, y, tokrow, colw, x2, mod, seq):
    t, d = x2.shape
    tb = ROUTE_BLOCK
    nblk = t // tb
    per_batch = seq // tb
    y3 = y.reshape(y.shape[0] // SLAB, SLAB, d)
    return pl.pallas_call(
        _combine_kernel,
        grid_spec=pltpu.PrefetchScalarGridSpec(
            num_scalar_prefetch=1,
            grid=(nblk,),
            in_specs=[
                pl.BlockSpec(memory_space=pl.ANY),
                pl.BlockSpec((1, SUBLANES, LOCAL_ROWS), lambda i, sp: (i, 0, 0)),
                pl.BlockSpec((1, LOCAL_ROWS, LANES), lambda i, sp: (i, 0, 0)),
                pl.BlockSpec((tb, d), lambda i, sp: (i, 0)),
                pl.BlockSpec((1, N_MOD, d), lambda i, sp: (i // per_batch, 0, 0)),
            ],
            out_specs=pl.BlockSpec((tb, d), lambda i, sp: (i, 0)),
            scratch_shapes=[
                pltpu.VMEM((2, LOCAL_SLABS, SLAB, d), F32),
                pltpu.SemaphoreType.DMA((2,)),
            ],
        ),
        out_shape=jax.ShapeDtypeStruct((t, d), F32),
        compiler_params=_params("arbitrary"),
        name="moe_combine",
    )(slab_pos, y3, tokrow, colw, x2, mod)


def _moe_layer(x2, mod, seq, layer, norm_g, router_w, router_b, w_gate, b_gate, w_up, b_up, w_down, b_down):
    d = x2.shape[1]
    xloc, tokrow, colw, cnt = _route_call(x2, mod, norm_g.reshape(1, d), router_w, router_b, seq)
    tile_expert, slab_src, slab_pos, num_tiles, max_tiles = _slab_plan(cnt[:, :, 0])
    y = _expert_call(tile_expert, slab_src, num_tiles, max_tiles, xloc, layer, w_gate, b_gate, w_up, b_up, w_down,
                     b_down)
    return _combine_call(slab_pos, y, tokrow, colw, x2, mod, seq)


def kernel(x, c, positions, ada_w, ada_b, norm1_g, norm2_g, w_in, q_norm_g, k_norm_g, attn_sink, conv_w, conv_b,
           a_log, dt_bias, ssm_d, ssm_norm_g, w_attn_o, w_ssm_o, w_out, router_w, router_b, exp_w_gate, exp_b_gate,
           exp_w_up, exp_b_up, exp_w_down, exp_b_down):
    bsz, seq, d = x.shape
    depth = ada_w.shape[0]
    assert d == D_MODEL and seq % 512 == 0 and (bsz * seq) % ROUTE_BLOCK == 0
    mod_all = _ada_call(c, ada_w, ada_b).reshape(depth, bsz, N_MOD, d)
    cos_t, sin_t = _rope_call(positions)
    x2 = x.reshape(bsz * seq, d)
    for l in range(depth):
        mod = mod_all[l]
        x2 = _mixer_layer(x2, mod, cos_t, sin_t, bsz, seq, norm1_g[l], w_in[l], q_norm_g[l], k_norm_g[l],
                          attn_sink[l], conv_w[l], conv_b[l], a_log[l], dt_bias[l], ssm_d[l], ssm_norm_g[l],
                          w_attn_o[l], w_ssm_o[l], w_out[l])
        x2 = _moe_layer(x2, mod, seq, l, norm2_g[l], router_w[l], router_b[l], exp_w_gate, exp_b_gate,
                        exp_w_up, exp_b_up, exp_w_down, exp_b_down)
    return x2.reshape(bsz, seq, d)
```

```python
import functools

import jax
import jax.numpy as jnp
import numpy as np
from jax import lax
from jax.experimental import pallas as pl
from jax.experimental.pallas import tpu as pltpu

F32 = jnp.float32
BF16 = jnp.bfloat16
HIGHEST = lax.Precision.HIGHEST

LANES = 128
SUBLANES = 8
VMEM_LIMIT = 56 * 1024 * 1024

D_MODEL = 1024
NORM_EPS = 1e-5
N_MOD = 6

HEAD_DIM = 64
N_Q_HEADS = 16
N_KV_HEADS = 4
ATTN_WIDTH = N_Q_HEADS * HEAD_DIM
KV_WIDTH = N_KV_HEADS * HEAD_DIM
WINDOW = 128
ATTN_BLOCK = 128
ROPE_THETA = 500000.0
ROPE_DIMS = HEAD_DIM // 4
ROPE_HALF = ROPE_DIMS // 2
NEG_BIG = -1e30

SSM_INNER = 2 * D_MODEL
SSM_HEAD_DIM = 64
SSM_HEADS = SSM_INNER // SSM_HEAD_DIM
SSM_GROUPS = 4
SSM_HPG = SSM_HEADS // SSM_GROUPS
SSM_STATE = 128
SSM_CONV = 5
SSM_CHUNK = 128
GROUP_WIDTH = SSM_INNER // SSM_GROUPS
DT_COLS = 2 * SSM_HPG

N_EXPERTS = 32
TOP_K = 4
SWIGLU_LIMIT = 7.0
SWIGLU_ALPHA = 1.702

COL_Q = 0
COL_GATE_ATTN = 1024
COL_GATE_SSM = 2048
COL_Z = 3072
COL_XS = 5120
COL_K = 7168
COL_V = 7424
COL_B = 7680
COL_C = 8192
PROJ_WIDTH = 8704
PROJ_N_TILE = PROJ_WIDTH // 4

_O_Q, _O_K, _O_V, _O_Z, _O_XBC, _O_DT, _O_GATES = 0, 1024, 1280, 1536, 3584, 6656, 6720


def _params(*sem):
    return pltpu.CompilerParams(dimension_semantics=sem, vmem_limit_bytes=VMEM_LIMIT)


def _prenorm_modulate(x, gain, shift, scale):
    ms = jnp.mean(x * x, axis=-1, keepdims=True)
    return (x * lax.rsqrt(ms + NORM_EPS) * gain) * (1.0 + scale) + shift


def _ada_kernel(c_ref, w_ref, b_ref, o_ref):
    c = c_ref[...]
    c_act = c * (1.0 / (1.0 + jnp.exp(-c)))
    o_ref[0] = jnp.dot(c_act, w_ref[0], precision=HIGHEST, preferred_element_type=F32) + b_ref[0]


def _ada_call(c, ada_w, ada_b):
    depth, d, n = ada_w.shape
    bsz = c.shape[0]
    tn = 1536
    return pl.pallas_call(
        _ada_kernel,
        grid=(depth, n // tn),
        in_specs=[
            pl.BlockSpec((bsz, d), lambda l, j: (0, 0)),
            pl.BlockSpec((1, d, tn), lambda l, j: (l, 0, j)),
            pl.BlockSpec((1, 1, tn), lambda l, j: (l, 0, j)),
        ],
        out_specs=pl.BlockSpec((1, bsz, tn), lambda l, j: (l, 0, j)),
        out_shape=jax.ShapeDtypeStruct((depth, bsz, n), F32),
        compiler_params=_params("arbitrary", "arbitrary"),
        name="ada_mod",
    )(c, ada_w, ada_b.reshape(depth, 1, n))


def _rope_kernel(pos_ref, freq_ref, sign_ref, cos_ref, sin_ref):
    ang = pos_ref[0].astype(F32) * freq_ref[...]
    sign = sign_ref[...]
    cos_ref[0] = jnp.where(sign == 0.0, 1.0, jnp.cos(ang))
    sin_ref[0] = jnp.sin(ang) * sign


def _rope_call(positions):
    bsz, s = positions.shape
    lane = np.arange(LANES) % HEAD_DIM
    inv_freq = ROPE_THETA ** (-np.arange(0, ROPE_DIMS, 2, dtype=np.float32) / ROPE_DIMS)
    freq = np.where(lane < ROPE_DIMS, inv_freq[lane % ROPE_HALF], 0.0).astype(np.float32)
    sign = np.where(lane < ROPE_HALF, -1.0, np.where(lane < ROPE_DIMS, 1.0, 0.0)).astype(np.float32)
    ts = 512
    spec = pl.BlockSpec((1, ts, LANES), lambda b, i: (b, i, 0))
    return pl.pallas_call(
        _rope_kernel,
        grid=(bsz, s // ts),
        in_specs=[
            pl.BlockSpec((1, ts, 1), lambda b, i: (b, i, 0)),
            pl.BlockSpec((1, LANES), lambda b, i: (0, 0)),
            pl.BlockSpec((1, LANES), lambda b, i: (0, 0)),
        ],
        out_specs=[spec, spec],
        out_shape=[jax.ShapeDtypeStruct((bsz, s, LANES), F32)] * 2,
        compiler_params=_params("arbitrary", "arbitrary"),
        name="rope_tables",
    )(positions.reshape(bsz, s, 1), jnp.asarray(freq).reshape(1, LANES), jnp.asarray(sign).reshape(1, LANES))


def _inproj_kernel(x_ref, mod_ref, g_ref, w_ref, o_ref, h_scr):
    @pl.when(pl.program_id(1) == 0)
    def _():
        rows = 256
        def fill(c, carry):
            r = pl.ds(pl.multiple_of(c * rows, rows), rows)
            h_scr[r, :] = _prenorm_modulate(x_ref[r, :], g_ref[...], mod_ref[0, 0:1, :], mod_ref[0, 1:2, :]).astype(BF16)
            return carry
        lax.fori_loop(0, h_scr.shape[0] // rows, fill, 0)

    o_ref[...] = jnp.dot(h_scr[...], w_ref[...], preferred_element_type=F32).astype(BF16)


def _inproj_call(x2, mod, gain, w_main, seq):
    t, d = x2.shape
    tm = 1024 if seq % 1024 == 0 else 512
    per_batch = seq // tm
    return pl.pallas_call(
        _inproj_kernel,
        grid=(t // tm, PROJ_WIDTH // PROJ_N_TILE),
        in_specs=[
            pl.BlockSpec((tm, d), lambda i, n: (i, 0)),
            pl.BlockSpec((1, N_MOD, d), lambda i, n: (i // per_batch, 0, 0)),
            pl.BlockSpec((1, d), lambda i, n: (0, 0)),
            pl.BlockSpec((d, PROJ_N_TILE), lambda i, n: (0, n)),
        ],
        out_specs=pl.BlockSpec((tm, PROJ_N_TILE), lambda i, n: (i, n)),
        out_shape=jax.ShapeDtypeStruct((t, PROJ_WIDTH), BF16),
        scratch_shapes=[pltpu.VMEM((tm, d), BF16)],
        compiler_params=_params("arbitrary", "arbitrary"),
        name="in_proj",
    )(x2, mod, gain, w_main)


PARTS = 3
LR_STRIDE = 8
DT_ROWS = 2 * PARTS
X_BLOCKS = ("ev_fwd", "wv_fwd", "ev_bwd", "wv_bwd")
X_STRIDE = PARTS * SSM_HPG
LOG_DT_FLOOR = -200.0
LOG2_E = 1.4426950408889634
SSD_UNROLL = 2
DT_CHUNKS_PER_STEP = 2
GW_ALL = SSM_GROUPS * LANES


def _placement_constants():
    place = np.zeros((5 * PARTS, LANES, GW_ALL), np.float32)
    ones = np.zeros((2, 1, GW_ALL), np.float32)
    for g in range(SSM_GROUPS):
        for j in range(DT_COLS):
            src = g * DT_COLS + j
            base = g * LANES + j * LR_STRIDE
            for k in range(PARTS):
                place[k, src, base + k] = 1.0
                place[PARTS + k, src, base + PARTS + k] = -1.0
            place[2 * PARTS, src, base + DT_ROWS] = 1.0
            place[2 * PARTS + 1, src, base + DT_ROWS + 1] = 1.0
            ones[0, 0, base + PARTS:base + 2 * PARTS] = 1.0
            ones[1, 0, base:base + PARTS] = 1.0
        for r in range(SSM_HPG):
            fwd, bwd = g * DT_COLS + r, g * DT_COLS + SSM_HPG + r
            for k in range(PARTS):
                col = g * LANES + k * SSM_HPG + r
                place[3 * PARTS + k, fwd, col + 0 * X_STRIDE] = 1.0
                place[3 * PARTS + k, bwd, col + 2 * X_STRIDE] = 1.0
                place[4 * PARTS + k, fwd, col + 1 * X_STRIDE] = 1.0
                place[4 * PARTS + k, bwd, col + 3 * X_STRIDE] = 1.0
    return jnp.asarray(place, BF16), jnp.asarray(ones, F32)


def _expansion_constant():
    ex = np.zeros((LANES, len(X_BLOCKS) * GROUP_WIDTH), np.float32)
    for t in range(len(X_BLOCKS)):
        for k in range(PARTS):
            for r in range(SSM_HPG):
                row = t * X_STRIDE + k * SSM_HPG + r
                ex[row, t * GROUP_WIDTH + r * SSM_HEAD_DIM:t * GROUP_WIDTH + (r + 1) * SSM_HEAD_DIM] = 1.0
    return jnp.asarray(ex, BF16)


def _split3(x):
    hi = x.astype(BF16)
    r1 = x - hi.astype(F32)
    mid = r1.astype(BF16)
    lo = (r1 - mid.astype(F32)).astype(BF16)
    return hi, mid, lo


def _dt_kernel(x_ref, mod_ref, g_ref, wh_ref, wl_ref, bias_ref, alog_ref, place_ref, ones_ref,
               lmat_ref, rmat_ref, xmat_ref, dch_ref):
    q = SSM_CHUNK
    rows = DT_CHUNKS_PER_STEP * q
    h = _prenorm_modulate(x_ref[...], g_ref[...], mod_ref[0, 0:1, :], mod_ref[0, 1:2, :])
    hb = h.astype(BF16)
    h_lo = (h - hb.astype(F32)).astype(BF16)
    raw = (jnp.dot(hb, wh_ref[...], preferred_element_type=F32) + jnp.dot(hb, wl_ref[...], preferred_element_type=F32)
           + jnp.dot(h_lo, wh_ref[...], preferred_element_type=F32)) + bias_ref[...]
    dt = jnp.maximum(raw, 0.0) + jnp.log1p(jnp.exp(-jnp.abs(raw)))
    da = dt * (-jnp.exp(alog_ref[...]))
    row = lax.broadcasted_iota(jnp.int32, (rows, rows), 0)
    col = lax.broadcasted_iota(jnp.int32, (rows, rows), 1)
    tri = jnp.where((row >= col) & (row // q == col // q), 1.0, 0.0).astype(BF16)
    prefix = None
    for part in _split3(da):
        term = jnp.dot(tri, part, preferred_element_type=F32)
        prefix = term if prefix is None else prefix + term
    chunk_of_row = lax.broadcasted_iota(jnp.int32, (rows, 1), 0) // q
    tot = jnp.zeros((rows, LANES), F32)
    for c in range(DT_CHUNKS_PER_STEP):
        last = prefix[(c + 1) * q - 1:(c + 1) * q, :]
        tot = jnp.where(chunk_of_row == c, last, tot)
        dch_ref[0, c] = jnp.exp(last)
    lane = lax.broadcasted_iota(jnp.int32, (1, LANES), 1)
    is_fwd = (lane % DT_COLS) < SSM_HPG
    cum = jnp.where(is_fwd, prefix, tot - prefix + da)
    ev = jnp.exp(cum)
    wv = jnp.exp(tot - cum) * dt
    cum2 = cum * LOG2_E
    rsub = cum2 - jnp.maximum(jnp.log(dt), LOG_DT_FLOOR) * LOG2_E
    diag = jnp.maximum(jnp.log(dt + pltpu.roll(dt, SSM_HPG, 1)), LOG_DT_FLOOR) * LOG2_E

    def placed(first, value, n_parts=PARTS):
        acc = None
        for k, part in enumerate(_split3(value)[:n_parts]):
            term = jnp.dot(part, place_ref[first + k], preferred_element_type=F32)
            acc = term if acc is None else acc + term
        return acc

    lmat_ref[...] = (placed(0, cum2) + ones_ref[0]).astype(BF16)
    rmat_ref[...] = (placed(PARTS, rsub) + placed(2 * PARTS, diag, 2) + ones_ref[1]).T.astype(BF16)
    xmat_ref[...] = (placed(3 * PARTS, ev) + placed(4 * PARTS, wv)).astype(BF16)


def _dt_call(x2, mod, gain, w_dt, dt_bias, a_log, bsz, seq):
    t, d = x2.shape
    per_step = DT_CHUNKS_PER_STEP
    q = SSM_CHUNK * per_step
    nc = seq // q
    place, ones = _placement_constants()
    w_hi = w_dt.astype(BF16)
    return pl.pallas_call(
        _dt_kernel,
        grid=(bsz, nc),
        in_specs=[
            pl.BlockSpec((q, d), lambda b, c: (b * nc + c, 0)),
            pl.BlockSpec((1, N_MOD, d), lambda b, c: (b, 0, 0)),
            pl.BlockSpec((1, d), lambda b, c: (0, 0)),
            pl.BlockSpec((d, LANES), lambda b, c: (0, 0)),
            pl.BlockSpec((d, LANES), lambda b, c: (0, 0)),
            pl.BlockSpec((1, LANES), lambda b, c: (0, 0)),
            pl.BlockSpec((1, LANES), lambda b, c: (0, 0)),
            pl.BlockSpec(place.shape, lambda b, c: (0, 0, 0)),
            pl.BlockSpec(ones.shape, lambda b, c: (0, 0, 0)),
        ],
        out_specs=[
            pl.BlockSpec((q, GW_ALL), lambda b, c: (b * nc + c, 0)),
            pl.BlockSpec((GW_ALL, q), lambda b, c: (b, c)),
            pl.BlockSpec((q, GW_ALL), lambda b, c: (b * nc + c, 0)),
            pl.BlockSpec((1, per_step, 1, LANES), lambda b, c: (b, c, 0, 0)),
        ],
        out_shape=[
            jax.ShapeDtypeStruct((t, GW_ALL), BF16),
            jax.ShapeDtypeStruct((bsz * GW_ALL, seq), BF16),
            jax.ShapeDtypeStruct((t, GW_ALL), BF16),
            jax.ShapeDtypeStruct((bsz, nc * per_step, 1, LANES), F32),
        ],
        compiler_params=_params("arbitrary", "arbitrary"),
        name="dt_prep",
    )(x2, mod, gain, w_hi, (w_dt - w_hi.astype(F32)).astype(BF16), dt_bias, a_log, place, ones)


def _head_rms_rope(t, gain, cos, sin, lane):
    lo = lane < HEAD_DIM
    sq = t * t
    ss_lo = jnp.sum(jnp.where(lo, sq, 0.0), axis=-1, keepdims=True)
    ss_hi = jnp.sum(jnp.where(lo, 0.0, sq), axis=-1, keepdims=True)
    r = jnp.where(lo, lax.rsqrt(ss_lo * (1.0 / HEAD_DIM) + NORM_EPS), lax.rsqrt(ss_hi * (1.0 / HEAD_DIM) + NORM_EPS))
    tn = t * r * gain
    first_half = (lane % HEAD_DIM) < ROPE_HALF
    partner = jnp.where(first_half, pltpu.roll(tn, LANES - ROPE_HALF, 1), pltpu.roll(tn, ROPE_HALF, 1))
    return tn * cos + partner * sin


def _head_rms_rope_mxu(t, gain, cos, sin, head_sum, swap):
    ss = jnp.dot((t * t).astype(BF16), head_sum, preferred_element_type=F32)
    tn = t * lax.rsqrt(ss * (1.0 / HEAD_DIM) + NORM_EPS) * gain
    partner = jnp.dot(tn.astype(BF16), swap, preferred_element_type=F32)
    return tn * cos + partner * sin


def _window_masks(seq):
    blk = ATTN_BLOCK
    key = np.arange(3 * blk)[:, None]
    row = np.arange(blk)[None, :]
    in_band = (key - row >= 0) & (key - row <= 2 * WINDOW)
    nb = seq // blk
    masks = []
    for n in (0, 1, nb - 1):
        kidx = key + (n - 1) * blk
        masks.append(np.where(in_band & (kidx >= 0) & (kidx < seq), 0.0, NEG_BIG))
    return jnp.asarray(np.stack(masks), BF16)


def _sum_columns():
    ones = np.zeros((6 * ATTN_BLOCK, LANES), np.float32)
    ones[:3 * ATTN_BLOCK, 0] = 1.0
    ones[3 * ATTN_BLOCK:, 1] = 1.0
    return jnp.asarray(ones, BF16)


def _attn_kernel(sink_ref, q_ref, k_ref, v_ref, cos_ref, sin_ref, qg_ref, kg_ref, mask_ref, ones_ref, o_ref,
                 kpad, vpad, sc_scr, p_scr):
    blk = ATTN_BLOCK
    s = q_ref.shape[0]
    nb = s // blk
    lane = lax.broadcasted_iota(jnp.int32, (1, LANES), 1)
    lo = lane < HEAD_DIM

    zero_blk = jnp.zeros((blk, LANES), BF16)
    for j in range(2 * N_KV_HEADS):
        kpad[j, 0:blk, :] = zero_blk
        kpad[j, blk + s:2 * blk + s, :] = zero_blk
        vpad[j, 0:blk, :] = zero_blk
        vpad[j, blk + s:2 * blk + s, :] = zero_blk

    def prep(c, carry):
        r0 = pl.multiple_of(c * blk, blk)
        rows = pl.ds(r0, blk)
        dst = pl.ds(r0 + blk, blk)
        cos = cos_ref[0, rows, :]
        sin = sin_ref[0, rows, :]
        for tpair in range(N_KV_HEADS // 2):
            ls = slice(tpair * LANES, (tpair + 1) * LANES)
            kr = _head_rms_rope(k_ref[rows, ls].astype(F32), kg_ref[...], cos, sin, lane)
            vv = v_ref[rows, ls].astype(F32)
            for src, store in ((kr, kpad), (vv, vpad)):
                even_lo = jnp.where(lo, src, 0.0)
                odd_hi = jnp.where(lo, 0.0, src)
                g0, g1 = 2 * tpair, 2 * tpair + 1
                store[2 * g0, dst, :] = even_lo.astype(BF16)
                store[2 * g0 + 1, dst, :] = pltpu.roll(even_lo, HEAD_DIM, 1).astype(BF16)
                store[2 * g1, dst, :] = pltpu.roll(odd_hi, HEAD_DIM, 1).astype(BF16)
                store[2 * g1 + 1, dst, :] = odd_hi.astype(BF16)
        return carry

    lax.fori_loop(0, nb, prep, 0)

    ri = lax.broadcasted_iota(jnp.int32, (blk, blk), 0)
    ci = lax.broadcasted_iota(jnp.int32, (blk, blk), 1)
    eye = jnp.where(ri == ci, 1.0, 0.0).astype(BF16)
    head_sum = jnp.where(ri // HEAD_DIM == ci // HEAD_DIM, 1.0, 0.0).astype(BF16)
    cj = ci % HEAD_DIM
    swap = jnp.where(((cj < ROPE_HALF) & (ri == ci + ROPE_HALF)) | ((cj >= ROPE_HALF) & (cj < ROPE_DIMS)
                                                                      & (ri == ci - ROPE_HALF)), 1.0, 0.0).astype(BF16)

    def qblock(n, carry):
        r0 = pl.multiple_of(n * blk, blk)
        rows = pl.ds(r0, blk)
        band = pl.ds(r0, 3 * blk)
        cos = cos_ref[0, rows, :]
        sin = sin_ref[0, rows, :]
        edge = jnp.where(n == 0, 0, jnp.where(n == nb - 1, 2, 1))
        mask_t = mask_ref[edge]
        n_pairs = N_Q_HEADS // 2
        qts = [q_ref[rows, pair * LANES:(pair + 1) * LANES].astype(F32) for pair in range(n_pairs)]
        sss = [jnp.dot((qt * qt).astype(BF16), head_sum, preferred_element_type=F32) for qt in qts]
        tns = [qt * lax.rsqrt(ss * (1.0 / HEAD_DIM) + NORM_EPS) * qg_ref[...] for qt, ss in zip(qts, sss)]
        partners = [jnp.dot(tn.astype(BF16), swap, preferred_element_type=F32) for tn in tns]
        for pair in range(n_pairs):
            g = pair // 2
            qq = (tns[pair] * cos + partners[pair] * sin).astype(BF16)
            lhs = jnp.concatenate([qq, eye], axis=1)
            keys = jnp.concatenate([jnp.concatenate([kpad[2 * g, band, :], mask_t], axis=1),
                                    jnp.concatenate([kpad[2 * g + 1, band, :], mask_t], axis=1)], axis=0)
            sc_scr[pair] = lax.dot_general(lhs, keys, (((1,), (1,)), ((), ())), preferred_element_type=F32)
        corr = []
        for pair in range(n_pairs):
            ms = []
            for par in range(2):
                cols = slice(par * 3 * blk, (par + 1) * 3 * blk)
                sh = sc_scr[pair, :, cols]
                sink = sink_ref[2 * pair + par]
                m = jnp.maximum(jnp.max(sh, axis=-1, keepdims=True), sink)
                p_scr[pair, :, cols] = jnp.exp2(sh - m).astype(BF16)
                ms.append(jnp.exp2(sink - m))
            corr.append(ms)
        for pair in range(n_pairs):
            g = pair // 2
            vals = jnp.concatenate([vpad[2 * g, band, :], vpad[2 * g + 1, band, :]], axis=0)
            out = jnp.dot(p_scr[pair], jnp.concatenate([vals, ones_ref[...]], axis=1), preferred_element_type=F32)
            inv_lo = 1.0 / (out[:, LANES:LANES + 1] + corr[pair][0])
            inv_hi = 1.0 / (out[:, LANES + 1:LANES + 2] + corr[pair][1])
            o_ref[rows, pair * LANES:(pair + 1) * LANES] = (out[:, 0:LANES] * jnp.where(lo, inv_lo, inv_hi)).astype(BF16)
        return carry

    lax.fori_loop(0, nb, qblock, 0)


def _attn_call(proj, cos_t, sin_t, q_gain, k_gain, sink, bsz, seq):
    t = proj.shape[0]
    masks = _window_masks(seq)
    ones = _sum_columns()
    return pl.pallas_call(
        _attn_kernel,
        grid_spec=pltpu.PrefetchScalarGridSpec(
            num_scalar_prefetch=1,
            grid=(bsz,),
            in_specs=[
                pl.BlockSpec((seq, ATTN_WIDTH), lambda b, sk: (b, COL_Q // ATTN_WIDTH)),
                pl.BlockSpec((seq, KV_WIDTH), lambda b, sk: (b, COL_K // KV_WIDTH)),
                pl.BlockSpec((seq, KV_WIDTH), lambda b, sk: (b, COL_V // KV_WIDTH)),
                pl.BlockSpec((1, seq, LANES), lambda b, sk: (b, 0, 0)),
                pl.BlockSpec((1, seq, LANES), lambda b, sk: (b, 0, 0)),
                pl.BlockSpec((1, LANES), lambda b, sk: (0, 0)),
                pl.BlockSpec((1, LANES), lambda b, sk: (0, 0)),
                pl.BlockSpec(masks.shape, lambda b, sk: (0, 0, 0)),
                pl.BlockSpec(ones.shape, lambda b, sk: (0, 0)),
            ],
            out_specs=pl.BlockSpec((seq, ATTN_WIDTH), lambda b, sk: (b, 0)),
            scratch_shapes=[
                pltpu.VMEM((2 * N_KV_HEADS, seq + 2 * ATTN_BLOCK, LANES), BF16),
                pltpu.VMEM((2 * N_KV_HEADS, seq + 2 * ATTN_BLOCK, LANES), BF16),
                pltpu.VMEM((N_Q_HEADS // 2, ATTN_BLOCK, 6 * ATTN_BLOCK), F32),
                pltpu.VMEM((N_Q_HEADS // 2, ATTN_BLOCK, 6 * ATTN_BLOCK), BF16),
            ],
        ),
        out_shape=jax.ShapeDtypeStruct((t, ATTN_WIDTH), BF16),
        compiler_params=_params("arbitrary"),
        name="window_attn",
    )(sink, proj, proj, proj, cos_t, sin_t, q_gain, k_gain, masks, ones)


def _ssd_kernel(dch_ref, xs_ref, b_ref, c_ref, z_ref, cwx_ref, cwb_ref, cwc_ref, cbx_ref, cbb_ref, cbc_ref,
                lmat_ref, rmat_ref, xmat_ref, ex_ref, dskip_ref, ng_ref, o_ref,
                upad, xs_c, b_c, c_c, y_acc, st, st_b, d_all, m_all):
    q = SSM_CHUNK
    s = xs_ref.shape[0]
    nc = s // q
    pad = 16
    halo = SSM_CONV // 2
    win = q + pad
    width = GROUP_WIDTH + 2 * SSM_STATE

    upad[0:pad, :] = jnp.zeros((pad, width), F32)
    upad[pad + s:2 * pad + s, :] = jnp.zeros((pad, width), F32)

    def fill(c, carry):
        r0 = pl.multiple_of(c * q, q)
        rows = pl.ds(r0, q)
        dst = pl.ds(r0 + pad, q)
        upad[dst, 0:GROUP_WIDTH] = xs_ref[rows, :].astype(F32)
        upad[dst, GROUP_WIDTH:GROUP_WIDTH + SSM_STATE] = b_ref[rows, :].astype(F32)
        upad[dst, GROUP_WIDTH + SSM_STATE:width] = c_ref[rows, :].astype(F32)
        return carry

    lax.fori_loop(0, nc, fill, 0)

    def conv_cols(r0, lo_col, n_col, w_ref, bias_ref, dst_ref):
        window = upad[pl.ds(r0 + pad - SUBLANES, win), lo_col:lo_col + n_col]
        acc = jnp.zeros((q, n_col), F32) + bias_ref[...]
        for k in range(SSM_CONV):
            shift = SUBLANES - halo + k
            acc = acc + pltpu.roll(window, win - shift, 0)[0:q] * w_ref[k:k + 1, :]
        dst_ref[pl.ds(r0, q), :] = acc * (1.0 / (1.0 + jnp.exp(-acc)))

    def conv(c, carry):
        r0 = pl.multiple_of(c * q, q)
        for j in range(GROUP_WIDTH // LANES):
            conv_cols(r0, j * LANES, LANES, cwx_ref.at[:, j * LANES:(j + 1) * LANES],
                      cbx_ref.at[:, j * LANES:(j + 1) * LANES], xs_c.at[:, j * LANES:(j + 1) * LANES])
        conv_cols(r0, GROUP_WIDTH, SSM_STATE, cwb_ref, cbb_ref, b_c)
        conv_cols(r0, GROUP_WIDTH + SSM_STATE, SSM_STATE, cwc_ref, cbc_ref, c_c)
        return carry

    lax.fori_loop(0, nc, conv, 0)

    row = lax.broadcasted_iota(jnp.int32, (q, q), 0)
    col = lax.broadcasted_iota(jnp.int32, (q, q), 1)
    below = row > col
    on_diag = row == col
    head_of_row = row // LR_STRIDE
    low_half = lax.broadcasted_iota(jnp.int32, (q, LANES), 1) < SSM_HEAD_DIM
    gw = GROUP_WIDTH

    def expand(xm, block):
        return jnp.dot(xm, ex_ref[:, block * gw:(block + 1) * gw], preferred_element_type=F32)

    def decay_rows(c, first):
        return jnp.concatenate(
            [jnp.full((1, SSM_HEAD_DIM), dch_ref[0, 0, c, first + h], F32) for h in range(SSM_HPG)], axis=-1)

    def rows_of(c):
        return pl.ds(pl.multiple_of(c * q, q), q)

    def pair_step(i, j, slot):
        d_scr, m_scr = d_all.at[slot], m_all.at[slot]
        ri, rj = rows_of(i), rows_of(j)
        xs_i, xs_j = xs_c[ri, :], xs_c[rj, :]
        b_i, b_j = b_c[ri, :].astype(BF16), b_c[rj, :].astype(BF16)
        c_i, c_j = c_c[ri, :].astype(BF16), c_c[rj, :].astype(BF16)
        lm, rm = lmat_ref[ri, :], rmat_ref[:, ri]
        xm_i, xm_j = xmat_ref[ri, :], xmat_ref[rj, :]
        state_f, state_b = st[...], st_b[...]
        zero_b = jnp.zeros((q, LANES), BF16)

        wv_f, wv_b = expand(xm_i, 1), expand(xm_j, 3)
        ev_f, ev_b = expand(xm_i, 0), expand(xm_j, 2)
        off_f = jnp.dot(c_i, state_f.astype(BF16), preferred_element_type=F32)
        off_b = jnp.dot(c_j, state_b.astype(BF16), preferred_element_type=F32)
        cb = lax.dot_general(c_i, b_i, (((1,), (1,)), ((), ())), preferred_element_type=F32)
        for h in range(SSM_HPG):
            rhs = jnp.concatenate([jnp.where(head_of_row == h, rm, zero_b),
                                   jnp.where(head_of_row == SSM_HPG + h, rm, zero_b)], axis=1)
            d_scr[:, 2 * h * q:2 * (h + 1) * q] = jnp.dot(lm, rhs, preferred_element_type=F32)

        upd_f = lax.dot_general(b_i, (xs_i * wv_f).astype(BF16), (((0,), (0,)), ((), ())), preferred_element_type=F32)
        upd_b = lax.dot_general(b_j, (xs_j * wv_b).astype(BF16), (((0,), (0,)), ((), ())), preferred_element_type=F32)
        st[...] = state_f * decay_rows(i, 0) + upd_f
        st_b[...] = state_b * decay_rows(j, SSM_HPG) + upd_b
        y_i = off_f * ev_f + xs_i * dskip_ref[...]
        y_j = off_b * ev_b

        for h in range(SSM_HPG):
            d_f = d_scr[:, 2 * h * q:(2 * h + 1) * q]
            d_b = d_scr[:, (2 * h + 1) * q:(2 * h + 2) * q]
            diag_row = (SSM_HPG + h) * LR_STRIDE + DT_ROWS
            diag = rm[diag_row:diag_row + 1, :].astype(F32) + rm[diag_row + 1:diag_row + 2, :].astype(F32)
            e = jnp.exp2(jnp.where(below, d_f, jnp.where(on_diag, diag, d_b)))
            m_scr[:, h * q:(h + 1) * q] = (cb * e).astype(BF16)
        xs_b16 = xs_i.astype(BF16)
        ys = []
        for pair in range(SSM_HPG // 2):
            xpair = xs_b16[:, pair * LANES:(pair + 1) * LANES]
            stacked = jnp.concatenate([jnp.where(low_half, xpair, zero_b), jnp.where(low_half, zero_b, xpair)], axis=0)
            ys.append(jnp.dot(m_scr[:, 2 * pair * q:2 * (pair + 1) * q], stacked, preferred_element_type=F32))
        return y_i + jnp.concatenate(ys, axis=-1), y_j

    def finish(c, y):
        rows = rows_of(c)
        zz = z_ref[rows, :].astype(F32)
        y = y * (zz * (1.0 / (1.0 + jnp.exp(-zz))))
        ms = jnp.mean(y * y, axis=-1, keepdims=True)
        o_ref[rows, :] = (y * lax.rsqrt(ms + NORM_EPS) * ng_ref[...]).astype(BF16)

    st[...] = jnp.zeros_like(st)
    st_b[...] = jnp.zeros_like(st_b)

    unroll = SSD_UNROLL

    def outward(t, carry):
        for u in range(unroll):
            i = t * unroll + u
            j = nc - 1 - i
            y_i, y_j = pair_step(i, j, u)
            y_acc[rows_of(i), :] = y_i
            y_acc[rows_of(j), :] = y_j
        return carry

    def inward(t, carry):
        for u in range(unroll):
            i = t * unroll + u
            j = nc - 1 - i
            y_i, y_j = pair_step(i, j, u)
            finish(i, y_acc[rows_of(i), :] + y_i)
            finish(j, y_acc[rows_of(j), :] + y_j)
        return carry

    lax.fori_loop(0, nc // (2 * unroll), outward, 0)
    lax.fori_loop(nc // (2 * unroll), nc // unroll, inward, 0)


def _ssd_call(proj, conv_w, conv_b, lmat, rmat, xmat, dch, d_skip, norm_g, bsz, seq):
    t = proj.shape[0]
    g_n = SSM_GROUPS
    q = SSM_CHUNK
    gw = GROUP_WIDTH
    width = GROUP_WIDTH + 2 * SSM_STATE
    ex = _expansion_constant()
    return pl.pallas_call(
        _ssd_kernel,
        grid=(bsz, g_n),
        in_specs=[
            pl.BlockSpec((1, 1, seq // q, DT_COLS), lambda b, g: (b, g, 0, 0), memory_space=pltpu.SMEM),
            pl.BlockSpec((seq, gw), lambda b, g: (b, COL_XS // gw + g)),
            pl.BlockSpec((seq, SSM_STATE), lambda b, g: (b, COL_B // SSM_STATE + g)),
            pl.BlockSpec((seq, SSM_STATE), lambda b, g: (b, COL_C // SSM_STATE + g)),
            pl.BlockSpec((seq, gw), lambda b, g: (b, COL_Z // gw + g)),
            pl.BlockSpec((SSM_CONV, gw), lambda b, g: (0, g)),
            pl.BlockSpec((SSM_CONV, SSM_STATE), lambda b, g: (0, SSM_INNER // SSM_STATE + g)),
            pl.BlockSpec((SSM_CONV, SSM_STATE), lambda b, g: (0, SSM_INNER // SSM_STATE + g_n + g)),
            pl.BlockSpec((1, gw), lambda b, g: (0, g)),
            pl.BlockSpec((1, SSM_STATE), lambda b, g: (0, SSM_INNER // SSM_STATE + g)),
            pl.BlockSpec((1, SSM_STATE), lambda b, g: (0, SSM_INNER // SSM_STATE + g_n + g)),
            pl.BlockSpec((seq, LANES), lambda b, g: (b, g)),
            pl.BlockSpec((LANES, seq), lambda b, g: (b * g_n + g, 0)),
            pl.BlockSpec((seq, LANES), lambda b, g: (b, g)),
            pl.BlockSpec(ex.shape, lambda b, g: (0, 0)),
            pl.BlockSpec((1, gw), lambda b, g: (0, g)),
            pl.BlockSpec((1, gw), lambda b, g: (0, g)),
        ],
        out_specs=pl.BlockSpec((seq, gw), lambda b, g: (b, g)),
        out_shape=jax.ShapeDtypeStruct((t, SSM_INNER), BF16),
        scratch_shapes=[
            pltpu.VMEM((seq + 32, width), F32),
            pltpu.VMEM((seq, gw), F32),
            pltpu.VMEM((seq, SSM_STATE), F32),
            pltpu.VMEM((seq, SSM_STATE), F32),
            pltpu.VMEM((seq, gw), F32),
            pltpu.VMEM((SSM_STATE, gw), F32),
            pltpu.VMEM((SSM_STATE, gw), F32),
            pltpu.VMEM((SSD_UNROLL, q, 2 * SSM_HPG * q), F32),
            pltpu.VMEM((SSD_UNROLL, q, SSM_HPG * q), BF16),
        ],
        compiler_params=_params("arbitrary", "arbitrary"),
        name="ssd_mixer",
    )(dch, proj, proj, proj, proj, conv_w, conv_w, conv_w, conv_b, conv_b, conv_b,
      lmat, rmat, xmat, ex, d_skip, norm_g)


def _outproj_kernel(attn_ref, ssm_ref, ga_ref, gs_ref, x_ref, mod_ref, wa_ref, ws_ref, wo_ref, o_ref):
    ya = jnp.dot(attn_ref[...], wa_ref[...], preferred_element_type=F32)
    ys = jnp.dot(ssm_ref[...], ws_ref[...], preferred_element_type=F32)
    ga = ga_ref[...].astype(F32)
    gs = gs_ref[...].astype(F32)
    merged = ya * (1.0 / (1.0 + jnp.exp(-ga))) + ys * (1.0 / (1.0 + jnp.exp(-gs)))
    y = jnp.dot(merged.astype(BF16), wo_ref[...], preferred_element_type=F32)
    o_ref[...] = x_ref[...] + mod_ref[0, 2:3, :] * y


def _outproj_call(attn, ssm, proj, x2, mod, w_attn_o, w_ssm_o, w_out, seq):
    t, d = x2.shape
    tm = 512
    per_batch = seq // tm
    const = lambda i: (0, 0)
    return pl.pallas_call(
        _outproj_kernel,
        grid=(t // tm,),
        in_specs=[
            pl.BlockSpec((tm, ATTN_WIDTH), lambda i: (i, 0)),
            pl.BlockSpec((tm, SSM_INNER), lambda i: (i, 0)),
            pl.BlockSpec((tm, d), lambda i: (i, COL_GATE_ATTN // d)),
            pl.BlockSpec((tm, d), lambda i: (i, COL_GATE_SSM // d)),
            pl.BlockSpec((tm, d), lambda i: (i, 0)),
            pl.BlockSpec((1, N_MOD, d), lambda i: (i // per_batch, 0, 0)),
            pl.BlockSpec((ATTN_WIDTH, d), const),
            pl.BlockSpec((SSM_INNER, d), const),
            pl.BlockSpec((d, d), const),
        ],
        out_specs=pl.BlockSpec((tm, d), lambda i: (i, 0)),
        out_shape=jax.ShapeDtypeStruct((t, d), F32),
        compiler_params=_params("arbitrary"),
        name="out_proj",
    )(attn, ssm, proj, proj, x2, mod, w_attn_o, w_ssm_o, w_out)


def _permute_in_proj(w_in):
    w_main = jnp.concatenate(
        [w_in[:, _O_Q:_O_K], w_in[:, _O_GATES:], w_in[:, _O_Z:_O_XBC], w_in[:, _O_XBC:_O_XBC + SSM_INNER],
         w_in[:, _O_K:_O_Z], w_in[:, _O_XBC + SSM_INNER:_O_DT]], axis=1).astype(BF16)
    return w_main


def _dt_permutation():
    idx = np.zeros(2 * SSM_HEADS, np.int32)
    for g in range(SSM_GROUPS):
        for d in range(2):
            for r in range(SSM_HPG):
                idx[g * DT_COLS + d * SSM_HPG + r] = d * SSM_HEADS + g * SSM_HPG + r
    return idx


def _pad_lanes(a):
    return jnp.pad(a, [(0, 0)] * (a.ndim - 1) + [(0, LANES - a.shape[-1])])


def _mixer_layer(x2, mod, cos_t, sin_t, bsz, seq, norm_g, w_in, q_norm_g, k_norm_g, attn_sink, conv_w, conv_b,
                 a_log, dt_bias, ssm_d, ssm_norm_g, w_attn_o, w_ssm_o, w_out):
    d = x2.shape[1]
    gain = norm_g.reshape(1, d)
    perm = _dt_permutation()
    w_dt = _pad_lanes(w_in[:, _O_DT:_O_GATES][:, perm])
    dt_b = _pad_lanes(dt_bias.reshape(-1)[perm].reshape(1, -1))
    a_lg = _pad_lanes(a_log.reshape(-1)[perm].reshape(1, -1))

    proj = _inproj_call(x2, mod, gain, _permute_in_proj(w_in), seq)
    lmat, rmat, xmat, dch = _dt_call(x2, mod, gain, w_dt, dt_b, a_lg, bsz, seq)
    nc = seq // SSM_CHUNK
    dch = dch[:, :, 0, :2 * SSM_HEADS].reshape(bsz, nc, SSM_GROUPS, DT_COLS).transpose(0, 2, 1, 3)

    attn = _attn_call(proj, cos_t, sin_t, jnp.tile(q_norm_g, 2).reshape(1, LANES) * (HEAD_DIM ** -0.5 * LOG2_E),
                      jnp.tile(k_norm_g, 2).reshape(1, LANES), attn_sink * LOG2_E, bsz, seq)
    ssm = _ssd_call(proj, conv_w, conv_b.reshape(1, -1), lmat, rmat, xmat, dch,
                    jnp.repeat(ssm_d, SSM_HEAD_DIM).reshape(1, SSM_INNER), ssm_norm_g.reshape(1, SSM_INNER),
                    bsz, seq)
    return _outproj_call(attn, ssm, proj, x2, mod, w_attn_o.astype(BF16), w_ssm_o.astype(BF16),
                         w_out.astype(BF16), seq)


ROUTE_BLOCK = 256
SLAB = SUBLANES
LOCAL_ROWS = ROUTE_BLOCK * TOP_K + ROUTE_BLOCK
LOCAL_SLABS = LOCAL_ROWS // SLAB
TILE_ROWS = 512
TILE_SLABS = TILE_ROWS // SLAB


U32 = jnp.uint32


def _pack_halves(a, b):
    return lax.bitcast_convert_type(b, U32) | (lax.bitcast_convert_type(a, U32) >> 16)


def _unpack_halves(p):
    return (lax.bitcast_convert_type(p << 16, F32), lax.bitcast_convert_type(p & jnp.uint32(0xFFFF0000), F32))


def _round_bf16(a):
    return a.astype(BF16).astype(F32)


def _route_kernel(x_ref, mod_ref, g_ref, rwh_ref, rwl_ref, rb_ref, xloc_ref, tokrow_ref, colw_ref, cnt_ref):
    tb = ROUTE_BLOCK
    h = _prenorm_modulate(x_ref[...], g_ref[...], mod_ref[0, 3:4, :], mod_ref[0, 4:5, :])
    hb = h.astype(BF16)
    h_lo = (h - hb.astype(F32)).astype(BF16)
    logits = (jnp.dot(hb, rwh_ref[...], preferred_element_type=F32)
              + jnp.dot(hb, rwl_ref[...], preferred_element_type=F32)
              + jnp.dot(h_lo, rwh_ref[...], preferred_element_type=F32)) + rb_ref[...]
    v = logits.T[0:N_EXPERTS, :]
    erow = lax.broadcasted_iota(jnp.int32, (N_EXPERTS, tb), 0)
    hots, tops = [], []
    for _ in range(TOP_K):
        mk = jnp.max(v, axis=0, keepdims=True)
        first = jnp.min(jnp.where(v == mk, erow, N_EXPERTS), axis=0, keepdims=True)
        hot = erow == first
        v = jnp.where(hot, -jnp.inf, v)
        hots.append(hot)
        tops.append(mk)
    ps = [jnp.exp(mk - tops[0]) for mk in tops]
    denom = ps[0] + ps[1] + ps[2] + ps[3]
    sel = jnp.zeros((N_EXPERTS, tb), F32)
    for hot in hots:
        sel = sel + jnp.where(hot, 1.0, 0.0)
    ti = lax.broadcasted_iota(jnp.int32, (tb, tb), 0)
    tj = lax.broadcasted_iota(jnp.int32, (tb, tb), 1)
    before = jnp.where(ti < tj, 1.0, 0.0).astype(BF16)
    rank = jnp.dot(sel.astype(BF16), before, preferred_element_type=F32)
    count = jnp.sum(sel, axis=1, keepdims=True)
    slabs = jnp.floor((count + (SLAB - 1)) * (1.0 / SLAB))
    ei = lax.broadcasted_iota(jnp.int32, (N_EXPERTS, N_EXPERTS), 0)
    ej = lax.broadcasted_iota(jnp.int32, (N_EXPERTS, N_EXPERTS), 1)
    earlier = jnp.where(ej < ei, 1.0, 0.0).astype(BF16)
    slabs_b = jnp.broadcast_to(slabs, (N_EXPERTS, tb))
    start = jnp.dot(earlier, slabs_b.astype(BF16), preferred_element_type=F32)
    dest = start * SLAB + rank
    dki = [jnp.sum(jnp.where(hot, dest, 0.0), axis=0, keepdims=True).astype(jnp.int32) for hot in hots]
    wks = [p / denom for p in ps]
    cnt_ref[0] = slabs_b[:, 0:LANES]
    tok1 = (lax.broadcasted_iota(jnp.int32, (SUBLANES, tb), 1) + 1).astype(BF16)
    for c in range(LOCAL_ROWS // tb):
        rs = slice(c * tb, (c + 1) * tb)
        ri = lax.broadcasted_iota(jnp.int32, (tb, tb), 0) + c * tb
        weighted = jnp.where(ri == dki[0], wks[0], jnp.where(ri == dki[1], wks[1], jnp.where(
            ri == dki[2], wks[2], jnp.where(ri == dki[3], wks[3], 0.0))))
        onehot = jnp.where(weighted > 0.0, 1.0, 0.0).astype(BF16)
        picked = jnp.dot(onehot, hb, preferred_element_type=F32)
        half = picked.shape[1] // 2
        xloc_ref[0, rs, :] = _pack_halves(picked[:, :half], picked[:, half:])
        colw_ref[0, rs, :] = jnp.broadcast_to(jnp.sum(weighted, axis=1, keepdims=True), (tb, LANES))
        tokrow_ref[0, :, rs] = lax.dot_general(tok1, onehot, (((1,), (1,)), ((), ())), preferred_element_type=F32)


def _route_call(x2, mod, gain, router_w, router_b, seq):
    t, d = x2.shape
    tb = ROUTE_BLOCK
    nblk = t // tb
    per_batch = seq // tb
    rw = _pad_lanes(router_w)
    rw_hi = rw.astype(BF16)
    return pl.pallas_call(
        _route_kernel,
        grid=(nblk,),
        in_specs=[
            pl.BlockSpec((tb, d), lambda i: (i, 0)),
            pl.BlockSpec((1, N_MOD, d), lambda i: (i // per_batch, 0, 0)),
            pl.BlockSpec((1, d), lambda i: (0, 0)),
            pl.BlockSpec((d, LANES), lambda i: (0, 0)),
            pl.BlockSpec((d, LANES), lambda i: (0, 0)),
            pl.BlockSpec((1, LANES), lambda i: (0, 0)),
        ],
        out_specs=[
            pl.BlockSpec((1, LOCAL_ROWS, d // 2), lambda i: (i, 0, 0)),
            pl.BlockSpec((1, SUBLANES, LOCAL_ROWS), lambda i: (i, 0, 0)),
            pl.BlockSpec((1, LOCAL_ROWS, LANES), lambda i: (i, 0, 0)),
            pl.BlockSpec((1, N_EXPERTS, LANES), lambda i: (i, 0, 0)),
        ],
        out_shape=[
            jax.ShapeDtypeStruct((nblk, LOCAL_ROWS, d // 2), U32),
            jax.ShapeDtypeStruct((nblk, SUBLANES, LOCAL_ROWS), F32),
            jax.ShapeDtypeStruct((nblk, LOCAL_ROWS, LANES), F32),
            jax.ShapeDtypeStruct((nblk, N_EXPERTS, LANES), F32),
        ],
        compiler_params=_params("arbitrary"),
        name="moe_route",
    )(x2, mod, gain, rw_hi, (rw - rw_hi.astype(F32)).astype(BF16), _pad_lanes(router_b.reshape(1, -1)))


def _slab_plan(slab_counts):
    nblk = slab_counts.shape[0]
    max_slabs = nblk * LOCAL_SLABS + N_EXPERTS * (TILE_SLABS - 1)
    max_tiles = -(-max_slabs // TILE_SLABS)
    c8 = slab_counts.astype(jnp.int32)
    local_start = jnp.cumsum(c8, axis=1) - c8
    per_expert = jnp.sum(c8, axis=0)
    tiles_e = (per_expert + TILE_SLABS - 1) // TILE_SLABS
    tile_start = jnp.cumsum(tiles_e) - tiles_e
    num_tiles = jnp.sum(tiles_e)
    expert_start = tile_start * TILE_SLABS
    block_off = jnp.cumsum(c8, axis=0) - c8
    seg_start = expert_start[None, :] + block_off

    sl = jnp.arange(LOCAL_SLABS, dtype=jnp.int32)[None, :, None]
    in_seg = (local_start[:, None, :] <= sl) & (sl < (local_start + c8)[:, None, :])
    slab_pos = jnp.sum(jnp.where(in_seg, (seg_start - local_start)[:, None, :] + sl, 0), axis=-1).reshape(-1)

    p = jnp.arange(max_tiles * TILE_SLABS, dtype=jnp.int32)[:, None]
    in_exp = (expert_start[None, :] <= p) & (p < (expert_start + per_expert)[None, :])
    off = p - jnp.sum(jnp.where(in_exp, expert_start[None, :], 0), axis=-1, keepdims=True)
    table = jnp.concatenate([block_off.T, c8.T, local_start.T], axis=1).astype(F32)
    picked = jnp.round(jnp.dot(in_exp.astype(F32), table, precision=HIGHEST)).astype(jnp.int32)
    boff, cnt, lst = picked[:, :nblk], picked[:, nblk:2 * nblk], picked[:, 2 * nblk:]
    in_blk = (boff <= off) & (off < boff + cnt)
    blk_base = jnp.arange(nblk, dtype=jnp.int32)[None, :] * LOCAL_SLABS
    slab_src = jnp.sum(jnp.where(in_blk, blk_base + lst - boff + off, 0), axis=-1)

    ti = jnp.arange(max_tiles, dtype=jnp.int32)
    tile_expert = jnp.clip(jnp.sum(tile_start[None, :] <= jnp.minimum(ti, num_tiles - 1)[:, None], axis=-1) - 1,
                           0, N_EXPERTS - 1)
    return tile_expert.astype(jnp.int32), slab_src.astype(jnp.int32), slab_pos.astype(jnp.int32), \
        num_tiles.reshape(1).astype(jnp.int32), max_tiles


def _expert_kernel(te_ref, src_ref, nt_ref, xloc_hbm, wg_ref, bg_ref, wu_ref, bu_ref, wd_ref, bd_ref, y_ref,
                   xbuf, sem, wg_s, wu_s, wd_s):
    i = pl.program_id(0)
    nt = nt_ref[0]

    def slab_copy(tile, slot, j):
        return pltpu.make_async_copy(xloc_hbm.at[src_ref[tile * TILE_SLABS + j]], xbuf.at[slot, j], sem.at[slot])

    def issue(tile, slot):
        for j in range(TILE_SLABS):
            slab_copy(tile, slot, j).start()

    @pl.when(i == 0)
    def _():
        issue(0, 0)

    @pl.when(i + 1 < nt)
    def _():
        issue(i + 1, (i + 1) % 2)

    @pl.when(i < nt)
    def _():
        slot = i % 2
        for j in range(TILE_SLABS):
            slab_copy(i, slot, j).wait()
        new_expert = jnp.logical_or(i == 0, te_ref[i] != te_ref[jnp.maximum(i - 1, 0)])

        @pl.when(new_expert)
        def _():
            rows = 128
            for src, dst in ((wg_ref, wg_s), (wu_ref, wu_s), (wd_ref, wd_s)):
                def cast(c, carry, src=src, dst=dst):
                    r = pl.ds(pl.multiple_of(c * rows, rows), rows)
                    dst[r, :] = src[0, 0, r, :].astype(BF16)
                    return carry
                lax.fori_loop(0, src.shape[2] // rows, cast, 0)

        x_lo, x_hi = _unpack_halves(xbuf[slot].reshape(TILE_ROWS, xbuf.shape[-1]))
        x = jnp.concatenate([x_lo.astype(BF16), x_hi.astype(BF16)], axis=1)
        gate = jnp.dot(x, wg_s[...], preferred_element_type=F32) + bg_ref[0, 0]
        up = jnp.dot(x, wu_s[...], preferred_element_type=F32) + bu_ref[0, 0]
        glu = jnp.minimum(gate, SWIGLU_LIMIT)
        lin = jnp.clip(up, -SWIGLU_LIMIT, SWIGLU_LIMIT)
        act = glu * (1.0 / (1.0 + jnp.exp(-SWIGLU_ALPHA * glu))) * (lin + 1.0)
        y = jnp.dot(act.astype(BF16), wd_s[...], preferred_element_type=F32) + bd_ref[0, 0]
        half = y.shape[1] // 2
        y_ref[...] = _pack_halves(_round_bf16(y[:, :half]), _round_bf16(y[:, half:]))

    @pl.when(i >= nt)
    def _():
        y_ref[...] = jnp.zeros_like(y_ref)


def _expert_call(tile_expert, slab_src, num_tiles, max_tiles, xloc, layer, w_gate, b_gate, w_up, b_up, w_down, b_down):
    nblk, _, packed = xloc.shape
    depth, _, d, ff = w_gate.shape
    xloc3 = xloc.reshape(nblk * LOCAL_SLABS, SLAB, packed)
    wspec = lambda k, n: pl.BlockSpec((1, 1, k, n), lambda i, te, ss, nt: (layer, te[i], 0, 0))
    return pl.pallas_call(
        _expert_kernel,
        grid_spec=pltpu.PrefetchScalarGridSpec(
            num_scalar_prefetch=3,
            grid=(max_tiles,),
            in_specs=[
                pl.BlockSpec(memory_space=pl.ANY),
                wspec(d, ff), wspec(1, ff), wspec(d, ff), wspec(1, ff), wspec(ff, d), wspec(1, d),
            ],
            out_specs=pl.BlockSpec((TILE_ROWS, d // 2), lambda i, te, ss, nt: (i, 0)),
            scratch_shapes=[
                pltpu.VMEM((2, TILE_SLABS, SLAB, packed), U32),
                pltpu.SemaphoreType.DMA((2,)),
                pltpu.VMEM((d, ff), BF16),
                pltpu.VMEM((d, ff), BF16),
                pltpu.VMEM((ff, d), BF16),
            ],
        ),
        out_shape=jax.ShapeDtypeStruct((max_tiles * TILE_ROWS, d // 2), U32),
        compiler_params=_params("arbitrary"),
        name="moe_experts",
    )(tile_expert, slab_src, num_tiles, xloc3, w_gate, b_gate.reshape(depth, N_EXPERTS, 1, ff), w_up,
      b_up.reshape(depth, N_EXPERTS, 1, ff), w_down, b_down.reshape(depth, N_EXPERTS, 1, d))


COMBINE_ISSUE_UNROLL = 8


def _combine_kernel(pos_ref, y_hbm, tokrow_ref, colw_ref, x_ref, mod_ref, o_ref, ybuf, sem):
    b = pl.program_id(0)
    nb = pl.num_programs(0)
    tb = ROUTE_BLOCK
    d = x_ref.shape[-1]

    def slab_copy(blk, slot, j):
        return pltpu.make_async_copy(y_hbm.at[pos_ref[blk * LOCAL_SLABS + j]], ybuf.at[slot, j], sem.at[slot])

    def issue(blk, slot):
        def body(jj, carry):
            for u in range(COMBINE_ISSUE_UNROLL):
                slab_copy(blk, slot, jj * COMBINE_ISSUE_UNROLL + u).start()
            return carry
        lax.fori_loop(0, LOCAL_SLABS // COMBINE_ISSUE_UNROLL, body, 0)

    @pl.when(b == 0)
    def _():
        issue(0, 0)

    @pl.when(b + 1 < nb)
    def _():
        issue(b + 1, (b + 1) % 2)

    slot = b % 2
    pltpu.make_async_copy(y_hbm.at[pl.ds(0, LOCAL_SLABS)], ybuf.at[slot], sem.at[slot]).wait()

    tok1 = (lax.broadcasted_iota(jnp.int32, (tb, tb), 0) + 1).astype(F32)
    acc = jnp.zeros((tb, d), F32)
    slabs_per_chunk = tb // SLAB
    for c in range(LOCAL_ROWS // tb):
        rs = slice(c * tb, (c + 1) * tb)
        y_lo, y_hi = _unpack_halves(ybuf[slot, c * slabs_per_chunk:(c + 1) * slabs_per_chunk].reshape(tb, d // 2))
        w = colw_ref[0, rs, :]
        yw = jnp.concatenate([part[:, t * LANES:(t + 1) * LANES] * w for part in (y_lo, y_hi)
                              for t in range(d // 2 // LANES)], axis=1).astype(BF16)
        unperm = jnp.where(tok1 == tokrow_ref[0, 0:1, rs], 1.0, 0.0).astype(BF16)
        acc = acc + jnp.dot(unperm, yw, preferred_element_type=F32)
    o_ref[...] = x_ref[...] + mod_ref[0, 5:6, :] * acc


def _combine_call(slab_pos, y, tokrow, colw, x2, mod, seq):
    t, d = x2.shape
    tb = ROUTE_BLOCK
    nblk = t // tb
    per_batch = seq // tb
    y3 = y.reshape(y.shape[0] // SLAB, SLAB, d // 2)
    return pl.pallas_call(
        _combine_kernel,
        grid_spec=pltpu.PrefetchScalarGridSpec(
            num_scalar_prefetch=1,
            grid=(nblk,),
            in_specs=[
                pl.BlockSpec(memory_space=pl.ANY),
                pl.BlockSpec((1, SUBLANES, LOCAL_ROWS), lambda i, sp: (i, 0, 0)),
                pl.BlockSpec((1, LOCAL_ROWS, LANES), lambda i, sp: (i, 0, 0)),
                pl.BlockSpec((tb, d), lambda i, sp: (i, 0)),
                pl.BlockSpec((1, N_MOD, d), lambda i, sp: (i // per_batch, 0, 0)),
            ],
            out_specs=pl.BlockSpec((tb, d), lambda i, sp: (i, 0)),
            scratch_shapes=[
                pltpu.VMEM((2, LOCAL_SLABS, SLAB, d // 2), U32),
                pltpu.SemaphoreType.DMA((2,)),
            ],
        ),
        out_shape=jax.ShapeDtypeStruct((t, d), F32),
        compiler_params=_params("arbitrary"),
        name="moe_combine",
    )(slab_pos, y3, tokrow, colw, x2, mod)


def _moe_layer(x2, mod, seq, layer, norm_g, router_w, router_b, w_gate, b_gate, w_up, b_up, w_down, b_down):
    d = x2.shape[1]
    xloc, tokrow, colw, cnt = _route_call(x2, mod, norm_g.reshape(1, d), router_w, router_b, seq)
    tile_expert, slab_src, slab_pos, num_tiles, max_tiles = _slab_plan(cnt[:, :, 0])
    y = _expert_call(tile_expert, slab_src, num_tiles, max_tiles, xloc, layer, w_gate, b_gate, w_up, b_up, w_down,
                     b_down)
    return _combine_call(slab_pos, y, tokrow, colw, x2, mod, seq)


def kernel(x, c, positions, ada_w, ada_b, norm1_g, norm2_g, w_in, q_norm_g, k_norm_g, attn_sink, conv_w, conv_b,
           a_log, dt_bias, ssm_d, ssm_norm_g, w_attn_o, w_ssm_o, w_out, router_w, router_b, exp_w_gate, exp_b_gate,
           exp_w_up, exp_b_up, exp_w_down, exp_b_down):
    bsz, seq, d = x.shape
    depth = ada_w.shape[0]
    assert d == D_MODEL and seq % 512 == 0 and (bsz * seq) % ROUTE_BLOCK == 0
    mod_all = _ada_call(c, ada_w, ada_b).reshape(depth, bsz, N_MOD, d)
    cos_t, sin_t = _rope_call(positions)
    x2 = x.reshape(bsz * seq, d)
    for l in range(depth):
        mod = mod_all[l]
        x2 = _mixer_layer(x2, mod, cos_t, sin_t, bsz, seq, norm1_g[l], w_in[l], q_norm_g[l], k_norm_g[l],
                          attn_sink[l], conv_w[l], conv_b[l], a_log[l], dt_bias[l], ssm_d[l], ssm_norm_g[l],
                          w_attn_o[l], w_ssm_o[l], w_out[l])
        x2 = _moe_layer(x2, mod, seq, l, norm2_g[l], router_w[l], router_b[l], exp_w_gate, exp_b_gate,
                        exp_w_up, exp_b_up, exp_w_down, exp_b_down)
    return x2.reshape(bsz, seq, d)
```

```python
import functools

import jax
import jax.numpy as jnp
import numpy as np
from jax import lax
from jax.experimental import pallas as pl
from jax.experimental.pallas import tpu as pltpu

F32 = jnp.float32
BF16 = jnp.bfloat16
HIGHEST = lax.Precision.HIGHEST

LANES = 128
SUBLANES = 8
VMEM_LIMIT = 56 * 1024 * 1024

D_MODEL = 1024
NORM_EPS = 1e-5
N_MOD = 6

HEAD_DIM = 64
N_Q_HEADS = 16
N_KV_HEADS = 4
ATTN_WIDTH = N_Q_HEADS * HEAD_DIM
KV_WIDTH = N_KV_HEADS * HEAD_DIM
WINDOW = 128
ATTN_BLOCK = 128
ROPE_THETA = 500000.0
ROPE_DIMS = HEAD_DIM // 4
ROPE_HALF = ROPE_DIMS // 2
NEG_BIG = -1e30

SSM_INNER = 2 * D_MODEL
SSM_HEAD_DIM = 64
SSM_HEADS = SSM_INNER // SSM_HEAD_DIM
SSM_GROUPS = 4
SSM_HPG = SSM_HEADS // SSM_GROUPS
SSM_STATE = 128
SSM_CONV = 5
SSM_CHUNK = 128
GROUP_WIDTH = SSM_INNER // SSM_GROUPS
DT_COLS = 2 * SSM_HPG

N_EXPERTS = 32
TOP_K = 4
SWIGLU_LIMIT = 7.0
SWIGLU_ALPHA = 1.702

COL_Q = 0
COL_GATE_ATTN = 1024
COL_GATE_SSM = 2048
COL_Z = 3072
COL_XS = 5120
COL_K = 7168
COL_V = 7424
COL_B = 7680
COL_C = 8192
PROJ_WIDTH = 8704
PROJ_N_TILE = PROJ_WIDTH // 4

_O_Q, _O_K, _O_V, _O_Z, _O_XBC, _O_DT, _O_GATES = 0, 1024, 1280, 1536, 3584, 6656, 6720


def _params(*sem):
    return pltpu.CompilerParams(dimension_semantics=sem, vmem_limit_bytes=VMEM_LIMIT)


def _prenorm_modulate(x, gain, shift, scale):
    ms = jnp.mean(x * x, axis=-1, keepdims=True)
    return (x * lax.rsqrt(ms + NORM_EPS) * gain) * (1.0 + scale) + shift


def _ada_kernel(c_ref, w_ref, b_ref, o_ref):
    c = c_ref[...]
    c_act = c * (1.0 / (1.0 + jnp.exp(-c)))
    o_ref[0] = jnp.dot(c_act, w_ref[0], precision=HIGHEST, preferred_element_type=F32) + b_ref[0]


def _ada_call(c, ada_w, ada_b):
    depth, d, n = ada_w.shape
    bsz = c.shape[0]
    tn = 1536
    return pl.pallas_call(
        _ada_kernel,
        grid=(depth, n // tn),
        in_specs=[
            pl.BlockSpec((bsz, d), lambda l, j: (0, 0)),
            pl.BlockSpec((1, d, tn), lambda l, j: (l, 0, j)),
            pl.BlockSpec((1, 1, tn), lambda l, j: (l, 0, j)),
        ],
        out_specs=pl.BlockSpec((1, bsz, tn), lambda l, j: (l, 0, j)),
        out_shape=jax.ShapeDtypeStruct((depth, bsz, n), F32),
        compiler_params=_params("arbitrary", "arbitrary"),
        name="ada_mod",
    )(c, ada_w, ada_b.reshape(depth, 1, n))


def _rope_kernel(pos_ref, freq_ref, sign_ref, cos_ref, sin_ref):
    ang = pos_ref[0].astype(F32) * freq_ref[...]
    sign = sign_ref[...]
    cos_ref[0] = jnp.where(sign == 0.0, 1.0, jnp.cos(ang))
    sin_ref[0] = jnp.sin(ang) * sign


def _rope_call(positions):
    bsz, s = positions.shape
    lane = np.arange(LANES) % HEAD_DIM
    inv_freq = ROPE_THETA ** (-np.arange(0, ROPE_DIMS, 2, dtype=np.float32) / ROPE_DIMS)
    freq = np.where(lane < ROPE_DIMS, inv_freq[lane % ROPE_HALF], 0.0).astype(np.float32)
    sign = np.where(lane < ROPE_HALF, -1.0, np.where(lane < ROPE_DIMS, 1.0, 0.0)).astype(np.float32)
    ts = 512
    spec = pl.BlockSpec((1, ts, LANES), lambda b, i: (b, i, 0))
    return pl.pallas_call(
        _rope_kernel,
        grid=(bsz, s // ts),
        in_specs=[
            pl.BlockSpec((1, ts, 1), lambda b, i: (b, i, 0)),
            pl.BlockSpec((1, LANES), lambda b, i: (0, 0)),
            pl.BlockSpec((1, LANES), lambda b, i: (0, 0)),
        ],
        out_specs=[spec, spec],
        out_shape=[jax.ShapeDtypeStruct((bsz, s, LANES), F32)] * 2,
        compiler_params=_params("arbitrary", "arbitrary"),
        name="rope_tables",
    )(positions.reshape(bsz, s, 1), jnp.asarray(freq).reshape(1, LANES), jnp.asarray(sign).reshape(1, LANES))


def _inproj_kernel(x_ref, mod_ref, g_ref, w_ref, o_ref, h_scr):
    @pl.when(pl.program_id(1) == 0)
    def _():
        rows = 256
        def fill(c, carry):
            r = pl.ds(pl.multiple_of(c * rows, rows), rows)
            h_scr[r, :] = _prenorm_modulate(x_ref[r, :], g_ref[...], mod_ref[0, 0:1, :], mod_ref[0, 1:2, :]).astype(BF16)
            return carry
        lax.fori_loop(0, h_scr.shape[0] // rows, fill, 0)

    o_ref[...] = jnp.dot(h_scr[...], w_ref[...], preferred_element_type=F32).astype(BF16)


def _inproj_call(x2, mod, gain, w_main, seq):
    t, d = x2.shape
    tm = 1024 if seq % 1024 == 0 else 512
    per_batch = seq // tm
    return pl.pallas_call(
        _inproj_kernel,
        grid=(t // tm, PROJ_WIDTH // PROJ_N_TILE),
        in_specs=[
            pl.BlockSpec((tm, d), lambda i, n: (i, 0)),
            pl.BlockSpec((1, N_MOD, d), lambda i, n: (i // per_batch, 0, 0)),
            pl.BlockSpec((1, d), lambda i, n: (0, 0)),
            pl.BlockSpec((d, PROJ_N_TILE), lambda i, n: (0, n)),
        ],
        out_specs=pl.BlockSpec((tm, PROJ_N_TILE), lambda i, n: (i, n)),
        out_shape=jax.ShapeDtypeStruct((t, PROJ_WIDTH), BF16),
        scratch_shapes=[pltpu.VMEM((tm, d), BF16)],
        compiler_params=_params("arbitrary", "arbitrary"),
        name="in_proj",
    )(x2, mod, gain, w_main)


PARTS = 3
LR_STRIDE = 8
DT_ROWS = 2 * PARTS
X_BLOCKS = ("ev_fwd", "wv_fwd", "ev_bwd", "wv_bwd")
X_STRIDE = PARTS * SSM_HPG
LOG_DT_FLOOR = -200.0
LOG2_E = 1.4426950408889634
SSD_UNROLL = 2
DT_CHUNKS_PER_STEP = 2
GW_ALL = SSM_GROUPS * LANES


def _placement_constants():
    place = np.zeros((5 * PARTS, LANES, GW_ALL), np.float32)
    ones = np.zeros((2, 1, GW_ALL), np.float32)
    for g in range(SSM_GROUPS):
        for j in range(DT_COLS):
            src = g * DT_COLS + j
            base = g * LANES + j * LR_STRIDE
            for k in range(PARTS):
                place[k, src, base + k] = 1.0
                place[PARTS + k, src, base + PARTS + k] = -1.0
            place[2 * PARTS, src, base + DT_ROWS] = 1.0
            place[2 * PARTS + 1, src, base + DT_ROWS + 1] = 1.0
            ones[0, 0, base + PARTS:base + 2 * PARTS] = 1.0
            ones[1, 0, base:base + PARTS] = 1.0
        for r in range(SSM_HPG):
            fwd, bwd = g * DT_COLS + r, g * DT_COLS + SSM_HPG + r
            for k in range(PARTS):
                col = g * LANES + k * SSM_HPG + r
                place[3 * PARTS + k, fwd, col + 0 * X_STRIDE] = 1.0
                place[3 * PARTS + k, bwd, col + 2 * X_STRIDE] = 1.0
                place[4 * PARTS + k, fwd, col + 1 * X_STRIDE] = 1.0
                place[4 * PARTS + k, bwd, col + 3 * X_STRIDE] = 1.0
    return jnp.asarray(place, BF16), jnp.asarray(ones, F32)


def _expansion_constant():
    ex = np.zeros((LANES, len(X_BLOCKS) * GROUP_WIDTH), np.float32)
    for t in range(len(X_BLOCKS)):
        for k in range(PARTS):
            for r in range(SSM_HPG):
                row = t * X_STRIDE + k * SSM_HPG + r
                ex[row, t * GROUP_WIDTH + r * SSM_HEAD_DIM:t * GROUP_WIDTH + (r + 1) * SSM_HEAD_DIM] = 1.0
    return jnp.asarray(ex, BF16)


def _split3(x):
    hi = x.astype(BF16)
    r1 = x - hi.astype(F32)
    mid = r1.astype(BF16)
    lo = (r1 - mid.astype(F32)).astype(BF16)
    return hi, mid, lo


def _dt_kernel(x_ref, mod_ref, g_ref, wh_ref, wl_ref, bias_ref, alog_ref, place_ref, ones_ref,
               lmat_ref, rmat_ref, xmat_ref, dch_ref):
    q = SSM_CHUNK
    rows = DT_CHUNKS_PER_STEP * q
    h = _prenorm_modulate(x_ref[...], g_ref[...], mod_ref[0, 0:1, :], mod_ref[0, 1:2, :])
    hb = h.astype(BF16)
    h_lo = (h - hb.astype(F32)).astype(BF16)
    raw = (jnp.dot(hb, wh_ref[...], preferred_element_type=F32) + jnp.dot(hb, wl_ref[...], preferred_element_type=F32)
           + jnp.dot(h_lo, wh_ref[...], preferred_element_type=F32)) + bias_ref[...]
    dt = jnp.maximum(raw, 0.0) + jnp.log1p(jnp.exp(-jnp.abs(raw)))
    da = dt * (-jnp.exp(alog_ref[...]))
    row = lax.broadcasted_iota(jnp.int32, (rows, rows), 0)
    col = lax.broadcasted_iota(jnp.int32, (rows, rows), 1)
    tri = jnp.where((row >= col) & (row // q == col // q), 1.0, 0.0).astype(BF16)
    prefix = None
    for part in _split3(da):
        term = jnp.dot(tri, part, preferred_element_type=F32)
        prefix = term if prefix is None else prefix + term
    chunk_of_row = lax.broadcasted_iota(jnp.int32, (rows, 1), 0) // q
    tot = jnp.zeros((rows, LANES), F32)
    for c in range(DT_CHUNKS_PER_STEP):
        last = prefix[(c + 1) * q - 1:(c + 1) * q, :]
        tot = jnp.where(chunk_of_row == c, last, tot)
        dch_ref[0, c] = jnp.exp(last)
    lane = lax.broadcasted_iota(jnp.int32, (1, LANES), 1)
    is_fwd = (lane % DT_COLS) < SSM_HPG
    cum = jnp.where(is_fwd, prefix, tot - prefix + da)
    ev = jnp.exp(cum)
    wv = jnp.exp(tot - cum) * dt
    cum2 = cum * LOG2_E
    rsub = cum2 - jnp.maximum(jnp.log(dt), LOG_DT_FLOOR) * LOG2_E
    diag = jnp.maximum(jnp.log(dt + pltpu.roll(dt, SSM_HPG, 1)), LOG_DT_FLOOR) * LOG2_E

    def placed(first, value, n_parts=PARTS):
        acc = None
        for k, part in enumerate(_split3(value)[:n_parts]):
            term = jnp.dot(part, place_ref[first + k], preferred_element_type=F32)
            acc = term if acc is None else acc + term
        return acc

    lmat_ref[...] = (placed(0, cum2) + ones_ref[0]).astype(BF16)
    rmat_ref[...] = (placed(PARTS, rsub) + placed(2 * PARTS, diag, 2) + ones_ref[1]).T.astype(BF16)
    xmat_ref[...] = (placed(3 * PARTS, ev) + placed(4 * PARTS, wv)).astype(BF16)


def _dt_call(x2, mod, gain, w_dt, dt_bias, a_log, bsz, seq):
    t, d = x2.shape
    per_step = DT_CHUNKS_PER_STEP
    q = SSM_CHUNK * per_step
    nc = seq // q
    place, ones = _placement_constants()
    w_hi = w_dt.astype(BF16)
    return pl.pallas_call(
        _dt_kernel,
        grid=(bsz, nc),
        in_specs=[
            pl.BlockSpec((q, d), lambda b, c: (b * nc + c, 0)),
            pl.BlockSpec((1, N_MOD, d), lambda b, c: (b, 0, 0)),
            pl.BlockSpec((1, d), lambda b, c: (0, 0)),
            pl.BlockSpec((d, LANES), lambda b, c: (0, 0)),
            pl.BlockSpec((d, LANES), lambda b, c: (0, 0)),
            pl.BlockSpec((1, LANES), lambda b, c: (0, 0)),
            pl.BlockSpec((1, LANES), lambda b, c: (0, 0)),
            pl.BlockSpec(place.shape, lambda b, c: (0, 0, 0)),
            pl.BlockSpec(ones.shape, lambda b, c: (0, 0, 0)),
        ],
        out_specs=[
            pl.BlockSpec((q, GW_ALL), lambda b, c: (b * nc + c, 0)),
            pl.BlockSpec((GW_ALL, q), lambda b, c: (b, c)),
            pl.BlockSpec((q, GW_ALL), lambda b, c: (b * nc + c, 0)),
            pl.BlockSpec((1, per_step, 1, LANES), lambda b, c: (b, c, 0, 0)),
        ],
        out_shape=[
            jax.ShapeDtypeStruct((t, GW_ALL), BF16),
            jax.ShapeDtypeStruct((bsz * GW_ALL, seq), BF16),
            jax.ShapeDtypeStruct((t, GW_ALL), BF16),
            jax.ShapeDtypeStruct((bsz, nc * per_step, 1, LANES), F32),
        ],
        compiler_params=_params("arbitrary", "arbitrary"),
        name="dt_prep",
    )(x2, mod, gain, w_hi, (w_dt - w_hi.astype(F32)).astype(BF16), dt_bias, a_log, place, ones)


def _head_rms_rope_mxu(t, gain, cos, sin, head_sum, swap):
    ss = jnp.dot((t * t).astype(BF16), head_sum, preferred_element_type=F32)
    tn = t * lax.rsqrt(ss * (1.0 / HEAD_DIM) + NORM_EPS) * gain
    partner = jnp.dot(tn.astype(BF16), swap, preferred_element_type=F32)
    return tn * cos + partner * sin


def _window_masks(seq):
    blk = ATTN_BLOCK
    key = np.arange(3 * blk)[:, None]
    row = np.arange(blk)[None, :]
    in_band = (key - row >= 0) & (key - row <= 2 * WINDOW)
    nb = seq // blk
    masks = []
    for n in (0, 1, nb - 1):
        kidx = key + (n - 1) * blk
        masks.append(np.where(in_band & (kidx >= 0) & (kidx < seq), 0.0, NEG_BIG))
    return jnp.asarray(np.stack(masks), BF16)


def _sum_columns():
    ones = np.zeros((6 * ATTN_BLOCK, LANES), np.float32)
    ones[:3 * ATTN_BLOCK, 0] = 1.0
    ones[3 * ATTN_BLOCK:, 1] = 1.0
    return jnp.asarray(ones, BF16)


def _attn_kernel(sink_ref, q_ref, k_ref, v_ref, cos_ref, sin_ref, qg_ref, kg_ref, mask_ref, ones_ref, o_ref,
                 kpad, vpad, sc_scr, p_scr):
    blk = ATTN_BLOCK
    s = q_ref.shape[0]
    nb = s // blk
    lane = lax.broadcasted_iota(jnp.int32, (1, LANES), 1)
    lo = lane < HEAD_DIM
    ri = lax.broadcasted_iota(jnp.int32, (blk, blk), 0)
    ci = lax.broadcasted_iota(jnp.int32, (blk, blk), 1)
    eye = jnp.where(ri == ci, 1.0, 0.0).astype(BF16)
    head_sum = jnp.where(ri // HEAD_DIM == ci // HEAD_DIM, 1.0, 0.0).astype(BF16)
    cj = ci % HEAD_DIM
    swap = jnp.where(((cj < ROPE_HALF) & (ri == ci + ROPE_HALF)) | ((cj >= ROPE_HALF) & (cj < ROPE_DIMS)
                                                                      & (ri == ci - ROPE_HALF)), 1.0, 0.0).astype(BF16)

    zero_blk = jnp.zeros((blk, LANES), BF16)
    for j in range(2 * N_KV_HEADS):
        kpad[j, 0:blk, :] = zero_blk
        kpad[j, blk + s:2 * blk + s, :] = zero_blk
        vpad[j, 0:blk, :] = zero_blk
        vpad[j, blk + s:2 * blk + s, :] = zero_blk

    def prep(c, carry):
        r0 = pl.multiple_of(c * blk, blk)
        rows = pl.ds(r0, blk)
        dst = pl.ds(r0 + blk, blk)
        cos = cos_ref[0, rows, :]
        sin = sin_ref[0, rows, :]
        for tpair in range(N_KV_HEADS // 2):
            ls = slice(tpair * LANES, (tpair + 1) * LANES)
            kr = _head_rms_rope_mxu(k_ref[rows, ls].astype(F32), kg_ref[...], cos, sin, head_sum, swap)
            vv = v_ref[rows, ls].astype(F32)
            for src, store in ((kr, kpad), (vv, vpad)):
                even_lo = jnp.where(lo, src, 0.0)
                odd_hi = jnp.where(lo, 0.0, src)
                g0, g1 = 2 * tpair, 2 * tpair + 1
                store[2 * g0, dst, :] = even_lo.astype(BF16)
                store[2 * g0 + 1, dst, :] = pltpu.roll(even_lo, HEAD_DIM, 1).astype(BF16)
                store[2 * g1, dst, :] = pltpu.roll(odd_hi, HEAD_DIM, 1).astype(BF16)
                store[2 * g1 + 1, dst, :] = odd_hi.astype(BF16)
        return carry

    lax.fori_loop(0, nb, prep, 0)

    def qblock(n, carry):
        r0 = pl.multiple_of(n * blk, blk)
        rows = pl.ds(r0, blk)
        band = pl.ds(r0, 3 * blk)
        cos = cos_ref[0, rows, :]
        sin = sin_ref[0, rows, :]
        edge = jnp.where(n == 0, 0, jnp.where(n == nb - 1, 2, 1))
        mask_t = mask_ref[edge]
        n_pairs = N_Q_HEADS // 2
        qts = [q_ref[rows, pair * LANES:(pair + 1) * LANES].astype(F32) for pair in range(n_pairs)]
        sss = [jnp.dot((qt * qt).astype(BF16), head_sum, preferred_element_type=F32) for qt in qts]
        tns = [qt * lax.rsqrt(ss * (1.0 / HEAD_DIM) + NORM_EPS) * qg_ref[...] for qt, ss in zip(qts, sss)]
        partners = [jnp.dot(tn.astype(BF16), swap, preferred_element_type=F32) for tn in tns]
        for pair in range(n_pairs):
            g = pair // 2
            qq = (tns[pair] * cos + partners[pair] * sin).astype(BF16)
            lhs = jnp.concatenate([qq, eye], axis=1)
            keys = jnp.concatenate([jnp.concatenate([kpad[2 * g, band, :], mask_t], axis=1),
                                    jnp.concatenate([kpad[2 * g + 1, band, :], mask_t], axis=1)], axis=0)
            sc_scr[pair] = lax.dot_general(lhs, keys, (((1,), (1,)), ((), ())), preferred_element_type=F32)
        corr = []
        for pair in range(n_pairs):
            ms = []
            for par in range(2):
                cols = slice(par * 3 * blk, (par + 1) * 3 * blk)
                sh = sc_scr[pair, :, cols]
                sink = sink_ref[2 * pair + par]
                m = jnp.maximum(jnp.max(sh, axis=-1, keepdims=True), sink)
                p_scr[pair, :, cols] = jnp.exp2(sh - m).astype(BF16)
                ms.append(jnp.exp2(sink - m))
            corr.append(ms)
        for pair in range(n_pairs):
            g = pair // 2
            vals = jnp.concatenate([vpad[2 * g, band, :], vpad[2 * g + 1, band, :]], axis=0)
            out = jnp.dot(p_scr[pair], jnp.concatenate([vals, ones_ref[...]], axis=1), preferred_element_type=F32)
            inv_lo = 1.0 / (out[:, LANES:LANES + 1] + corr[pair][0])
            inv_hi = 1.0 / (out[:, LANES + 1:LANES + 2] + corr[pair][1])
            o_ref[rows, pair * LANES:(pair + 1) * LANES] = (out[:, 0:LANES] * jnp.where(lo, inv_lo, inv_hi)).astype(BF16)
        return carry

    lax.fori_loop(0, nb, qblock, 0)


def _attn_call(proj, cos_t, sin_t, q_gain, k_gain, sink, bsz, seq):
    t = proj.shape[0]
    masks = _window_masks(seq)
    ones = _sum_columns()
    return pl.pallas_call(
        _attn_kernel,
        grid_spec=pltpu.PrefetchScalarGridSpec(
            num_scalar_prefetch=1,
            grid=(bsz,),
            in_specs=[
                pl.BlockSpec((seq, ATTN_WIDTH), lambda b, sk: (b, COL_Q // ATTN_WIDTH)),
                pl.BlockSpec((seq, KV_WIDTH), lambda b, sk: (b, COL_K // KV_WIDTH)),
                pl.BlockSpec((seq, KV_WIDTH), lambda b, sk: (b, COL_V // KV_WIDTH)),
                pl.BlockSpec((1, seq, LANES), lambda b, sk: (b, 0, 0)),
                pl.BlockSpec((1, seq, LANES), lambda b, sk: (b, 0, 0)),
                pl.BlockSpec((1, LANES), lambda b, sk: (0, 0)),
                pl.BlockSpec((1, LANES), lambda b, sk: (0, 0)),
                pl.BlockSpec(masks.shape, lambda b, sk: (0, 0, 0)),
                pl.BlockSpec(ones.shape, lambda b, sk: (0, 0)),
            ],
            out_specs=pl.BlockSpec((seq, ATTN_WIDTH), lambda b, sk: (b, 0)),
            scratch_shapes=[
                pltpu.VMEM((2 * N_KV_HEADS, seq + 2 * ATTN_BLOCK, LANES), BF16),
                pltpu.VMEM((2 * N_KV_HEADS, seq + 2 * ATTN_BLOCK, LANES), BF16),
                pltpu.VMEM((N_Q_HEADS // 2, ATTN_BLOCK, 6 * ATTN_BLOCK), F32),
                pltpu.VMEM((N_Q_HEADS // 2, ATTN_BLOCK, 6 * ATTN_BLOCK), BF16),
            ],
        ),
        out_shape=jax.ShapeDtypeStruct((t, ATTN_WIDTH), BF16),
        compiler_params=_params("arbitrary"),
        name="window_attn",
    )(sink, proj, proj, proj, cos_t, sin_t, q_gain, k_gain, masks, ones)


def _ssd_kernel(dch_ref, xs_ref, b_ref, c_ref, z_ref, cwx_ref, cwb_ref, cwc_ref, cbx_ref, cbb_ref, cbc_ref,
                lmat_ref, rmat_ref, xmat_ref, ex_ref, dskip_ref, ng_ref, o_ref,
                upad, xs_c, b_c, c_c, y_acc, st, st_b, d_all, m_all):
    q = SSM_CHUNK
    s = xs_ref.shape[0]
    nc = s // q
    pad = 16
    halo = SSM_CONV // 2
    win = q + pad
    width = GROUP_WIDTH + 2 * SSM_STATE

    upad[0:pad, :] = jnp.zeros((pad, width), F32)
    upad[pad + s:2 * pad + s, :] = jnp.zeros((pad, width), F32)

    def fill(c, carry):
        r0 = pl.multiple_of(c * q, q)
        rows = pl.ds(r0, q)
        dst = pl.ds(r0 + pad, q)
        upad[dst, 0:GROUP_WIDTH] = xs_ref[rows, :].astype(F32)
        upad[dst, GROUP_WIDTH:GROUP_WIDTH + SSM_STATE] = b_ref[rows, :].astype(F32)
        upad[dst, GROUP_WIDTH + SSM_STATE:width] = c_ref[rows, :].astype(F32)
        return carry

    lax.fori_loop(0, nc, fill, 0)

    def conv_cols(r0, lo_col, n_col, w_ref, bias_ref, dst_ref):
        window = upad[pl.ds(r0 + pad - SUBLANES, win), lo_col:lo_col + n_col]
        acc = jnp.zeros((q, n_col), F32) + bias_ref[...]
        for k in range(SSM_CONV):
            shift = SUBLANES - halo + k
            acc = acc + pltpu.roll(window, win - shift, 0)[0:q] * w_ref[k:k + 1, :]
        dst_ref[pl.ds(r0, q), :] = acc * (1.0 / (1.0 + jnp.exp(-acc)))

    def conv(c, carry):
        r0 = pl.multiple_of(c * q, q)
        for j in range(GROUP_WIDTH // LANES):
            conv_cols(r0, j * LANES, LANES, cwx_ref.at[:, j * LANES:(j + 1) * LANES],
                      cbx_ref.at[:, j * LANES:(j + 1) * LANES], xs_c.at[:, j * LANES:(j + 1) * LANES])
        conv_cols(r0, GROUP_WIDTH, SSM_STATE, cwb_ref, cbb_ref, b_c)
        conv_cols(r0, GROUP_WIDTH + SSM_STATE, SSM_STATE, cwc_ref, cbc_ref, c_c)
        return carry

    lax.fori_loop(0, nc, conv, 0)

    row = lax.broadcasted_iota(jnp.int32, (q, q), 0)
    col = lax.broadcasted_iota(jnp.int32, (q, q), 1)
    below = row > col
    on_diag = row == col
    head_of_row = row // LR_STRIDE
    low_half = lax.broadcasted_iota(jnp.int32, (q, LANES), 1) < SSM_HEAD_DIM
    gw = GROUP_WIDTH

    def expand(xm, block):
        return jnp.dot(xm, ex_ref[:, block * gw:(block + 1) * gw], preferred_element_type=F32)

    def decay_rows(c, first):
        return jnp.concatenate(
            [jnp.full((1, SSM_HEAD_DIM), dch_ref[0, 0, c, first + h], F32) for h in range(SSM_HPG)], axis=-1)

    def rows_of(c):
        return pl.ds(pl.multiple_of(c * q, q), q)

    def pair_step(i, j, slot):
        d_scr, m_scr = d_all.at[slot], m_all.at[slot]
        ri, rj = rows_of(i), rows_of(j)
        xs_i, xs_j = xs_c[ri, :], xs_c[rj, :]
        b_i, b_j = b_c[ri, :].astype(BF16), b_c[rj, :].astype(BF16)
        c_i, c_j = c_c[ri, :].astype(BF16), c_c[rj, :].astype(BF16)
        lm, rm = lmat_ref[ri, :], rmat_ref[:, ri]
        xm_i, xm_j = xmat_ref[ri, :], xmat_ref[rj, :]
        state_f, state_b = st[...], st_b[...]
        zero_b = jnp.zeros((q, LANES), BF16)

        wv_f, wv_b = expand(xm_i, 1), expand(xm_j, 3)
        ev_f, ev_b = expand(xm_i, 0), expand(xm_j, 2)
        off_f = jnp.dot(c_i, state_f.astype(BF16), preferred_element_type=F32)
        off_b = jnp.dot(c_j, state_b.astype(BF16), preferred_element_type=F32)
        cb = lax.dot_general(c_i, b_i, (((1,), (1,)), ((), ())), preferred_element_type=F32)
        for h in range(SSM_HPG):
            rhs = jnp.concatenate([jnp.where(head_of_row == h, rm, zero_b),
                                   jnp.where(head_of_row == SSM_HPG + h, rm, zero_b)], axis=1)
            d_scr[:, 2 * h * q:2 * (h + 1) * q] = jnp.dot(lm, rhs, preferred_element_type=F32)

        upd_f = lax.dot_general(b_i, (xs_i * wv_f).astype(BF16), (((0,), (0,)), ((), ())), preferred_element_type=F32)
        upd_b = lax.dot_general(b_j, (xs_j * wv_b).astype(BF16), (((0,), (0,)), ((), ())), preferred_element_type=F32)
        st[...] = state_f * decay_rows(i, 0) + upd_f
        st_b[...] = state_b * decay_rows(j, SSM_HPG) + upd_b
        y_i = off_f * ev_f + xs_i * dskip_ref[...]
        y_j = off_b * ev_b

        for h in range(SSM_HPG):
            d_f = d_scr[:, 2 * h * q:(2 * h + 1) * q]
            d_b = d_scr[:, (2 * h + 1) * q:(2 * h + 2) * q]
            diag_row = (SSM_HPG + h) * LR_STRIDE + DT_ROWS
            diag = rm[diag_row:diag_row + 1, :].astype(F32) + rm[diag_row + 1:diag_row + 2, :].astype(F32)
            e = jnp.exp2(jnp.where(below, d_f, jnp.where(on_diag, diag, d_b)))
            m_scr[:, h * q:(h + 1) * q] = (cb * e).astype(BF16)
        xs_b16 = xs_i.astype(BF16)
        ys = []
        for pair in range(SSM_HPG // 2):
            xpair = xs_b16[:, pair * LANES:(pair + 1) * LANES]
            stacked = jnp.concatenate([jnp.where(low_half, xpair, zero_b), jnp.where(low_half, zero_b, xpair)], axis=0)
            ys.append(jnp.dot(m_scr[:, 2 * pair * q:2 * (pair + 1) * q], stacked, preferred_element_type=F32))
        return y_i + jnp.concatenate(ys, axis=-1), y_j

    def finish(c, y):
        rows = rows_of(c)
        zz = z_ref[rows, :].astype(F32)
        y = y * (zz * (1.0 / (1.0 + jnp.exp(-zz))))
        ms = jnp.mean(y * y, axis=-1, keepdims=True)
        o_ref[rows, :] = (y * lax.rsqrt(ms + NORM_EPS) * ng_ref[...]).astype(BF16)

    st[...] = jnp.zeros_like(st)
    st_b[...] = jnp.zeros_like(st_b)

    unroll = SSD_UNROLL

    def outward(t, carry):
        for u in range(unroll):
            i = t * unroll + u
            j = nc - 1 - i
            y_i, y_j = pair_step(i, j, u)
            y_acc[rows_of(i), :] = y_i
            y_acc[rows_of(j), :] = y_j
        return carry

    def inward(t, carry):
        for u in range(unroll):
            i = t * unroll + u
            j = nc - 1 - i
            y_i, y_j = pair_step(i, j, u)
            finish(i, y_acc[rows_of(i), :] + y_i)
            finish(j, y_acc[rows_of(j), :] + y_j)
        return carry

    lax.fori_loop(0, nc // (2 * unroll), outward, 0)
    lax.fori_loop(nc // (2 * unroll), nc // unroll, inward, 0)


def _ssd_call(proj, conv_w, conv_b, lmat, rmat, xmat, dch, d_skip, norm_g, bsz, seq):
    t = proj.shape[0]
    g_n = SSM_GROUPS
    q = SSM_CHUNK
    gw = GROUP_WIDTH
    width = GROUP_WIDTH + 2 * SSM_STATE
    ex = _expansion_constant()
    return pl.pallas_call(
        _ssd_kernel,
        grid=(bsz, g_n),
        in_specs=[
            pl.BlockSpec((1, 1, seq // q, DT_COLS), lambda b, g: (b, g, 0, 0), memory_space=pltpu.SMEM),
            pl.BlockSpec((seq, gw), lambda b, g: (b, COL_XS // gw + g)),
            pl.BlockSpec((seq, SSM_STATE), lambda b, g: (b, COL_B // SSM_STATE + g)),
            pl.BlockSpec((seq, SSM_STATE), lambda b, g: (b, COL_C // SSM_STATE + g)),
            pl.BlockSpec((seq, gw), lambda b, g: (b, COL_Z // gw + g)),
            pl.BlockSpec((SSM_CONV, gw), lambda b, g: (0, g)),
            pl.BlockSpec((SSM_CONV, SSM_STATE), lambda b, g: (0, SSM_INNER // SSM_STATE + g)),
            pl.BlockSpec((SSM_CONV, SSM_STATE), lambda b, g: (0, SSM_INNER // SSM_STATE + g_n + g)),
            pl.BlockSpec((1, gw), lambda b, g: (0, g)),
            pl.BlockSpec((1, SSM_STATE), lambda b, g: (0, SSM_INNER // SSM_STATE + g)),
            pl.BlockSpec((1, SSM_STATE), lambda b, g: (0, SSM_INNER // SSM_STATE + g_n + g)),
            pl.BlockSpec((seq, LANES), lambda b, g: (b, g)),
            pl.BlockSpec((LANES, seq), lambda b, g: (b * g_n + g, 0)),
            pl.BlockSpec((seq, LANES), lambda b, g: (b, g)),
            pl.BlockSpec(ex.shape, lambda b, g: (0, 0)),
            pl.BlockSpec((1, gw), lambda b, g: (0, g)),
            pl.BlockSpec((1, gw), lambda b, g: (0, g)),
        ],
        out_specs=pl.BlockSpec((seq, gw), lambda b, g: (b, g)),
        out_shape=jax.ShapeDtypeStruct((t, SSM_INNER), BF16),
        scratch_shapes=[
            pltpu.VMEM((seq + 32, width), F32),
            pltpu.VMEM((seq, gw), F32),
            pltpu.VMEM((seq, SSM_STATE), F32),
            pltpu.VMEM((seq, SSM_STATE), F32),
            pltpu.VMEM((seq, gw), F32),
            pltpu.VMEM((SSM_STATE, gw), F32),
            pltpu.VMEM((SSM_STATE, gw), F32),
            pltpu.VMEM((SSD_UNROLL, q, 2 * SSM_HPG * q), F32),
            pltpu.VMEM((SSD_UNROLL, q, SSM_HPG * q), BF16),
        ],
        compiler_params=_params("arbitrary", "arbitrary"),
        name="ssd_mixer",
    )(dch, proj, proj, proj, proj, conv_w, conv_w, conv_w, conv_b, conv_b, conv_b,
      lmat, rmat, xmat, ex, d_skip, norm_g)


def _outproj_kernel(attn_ref, ssm_ref, ga_ref, gs_ref, x_ref, mod_ref, wa_ref, ws_ref, wo_ref, o_ref):
    ya = jnp.dot(attn_ref[...], wa_ref[...], preferred_element_type=F32)
    ys = jnp.dot(ssm_ref[...], ws_ref[...], preferred_element_type=F32)
    ga = ga_ref[...].astype(F32)
    gs = gs_ref[...].astype(F32)
    merged = ya * (1.0 / (1.0 + jnp.exp(-ga))) + ys * (1.0 / (1.0 + jnp.exp(-gs)))
    y = jnp.dot(merged.astype(BF16), wo_ref[...], preferred_element_type=F32)
    o_ref[...] = x_ref[...] + mod_ref[0, 2:3, :] * y


def _outproj_call(attn, ssm, proj, x2, mod, w_attn_o, w_ssm_o, w_out, seq):
    t, d = x2.shape
    tm = 512
    per_batch = seq // tm
    const = lambda i: (0, 0)
    return pl.pallas_call(
        _outproj_kernel,
        grid=(t // tm,),
        in_specs=[
            pl.BlockSpec((tm, ATTN_WIDTH), lambda i: (i, 0)),
            pl.BlockSpec((tm, SSM_INNER), lambda i: (i, 0)),
            pl.BlockSpec((tm, d), lambda i: (i, COL_GATE_ATTN // d)),
            pl.BlockSpec((tm, d), lambda i: (i, COL_GATE_SSM // d)),
            pl.BlockSpec((tm, d), lambda i: (i, 0)),
            pl.BlockSpec((1, N_MOD, d), lambda i: (i // per_batch, 0, 0)),
            pl.BlockSpec((ATTN_WIDTH, d), const),
            pl.BlockSpec((SSM_INNER, d), const),
            pl.BlockSpec((d, d), const),
        ],
        out_specs=pl.BlockSpec((tm, d), lambda i: (i, 0)),
        out_shape=jax.ShapeDtypeStruct((t, d), F32),
        compiler_params=_params("arbitrary"),
        name="out_proj",
    )(attn, ssm, proj, proj, x2, mod, w_attn_o, w_ssm_o, w_out)


def _permute_in_proj(w_in):
    w_main = jnp.concatenate(
        [w_in[:, _O_Q:_O_K], w_in[:, _O_GATES:], w_in[:, _O_Z:_O_XBC], w_in[:, _O_XBC:_O_XBC + SSM_INNER],
         w_in[:, _O_K:_O_Z], w_in[:, _O_XBC + SSM_INNER:_O_DT]], axis=1).astype(BF16)
    return w_main


def _dt_permutation():
    idx = np.zeros(2 * SSM_HEADS, np.int32)
    for g in range(SSM_GROUPS):
        for d in range(2):
            for r in range(SSM_HPG):
                idx[g * DT_COLS + d * SSM_HPG + r] = d * SSM_HEADS + g * SSM_HPG + r
    return idx


def _pad_lanes(a):
    return jnp.pad(a, [(0, 0)] * (a.ndim - 1) + [(0, LANES - a.shape[-1])])


def _mixer_layer(x2, mod, cos_t, sin_t, bsz, seq, norm_g, w_in, q_norm_g, k_norm_g, attn_sink, conv_w, conv_b,
                 a_log, dt_bias, ssm_d, ssm_norm_g, w_attn_o, w_ssm_o, w_out):
    d = x2.shape[1]
    gain = norm_g.reshape(1, d)
    perm = _dt_permutation()
    w_dt = _pad_lanes(w_in[:, _O_DT:_O_GATES][:, perm])
    dt_b = _pad_lanes(dt_bias.reshape(-1)[perm].reshape(1, -1))
    a_lg = _pad_lanes(a_log.reshape(-1)[perm].reshape(1, -1))

    proj = _inproj_call(x2, mod, gain, _permute_in_proj(w_in), seq)
    lmat, rmat, xmat, dch = _dt_call(x2, mod, gain, w_dt, dt_b, a_lg, bsz, seq)
    nc = seq // SSM_CHUNK
    dch = dch[:, :, 0, :2 * SSM_HEADS].reshape(bsz, nc, SSM_GROUPS, DT_COLS).transpose(0, 2, 1, 3)

    attn = _attn_call(proj, cos_t, sin_t, jnp.tile(q_norm_g, 2).reshape(1, LANES) * (HEAD_DIM ** -0.5 * LOG2_E),
                      jnp.tile(k_norm_g, 2).reshape(1, LANES), attn_sink * LOG2_E, bsz, seq)
    ssm = _ssd_call(proj, conv_w, conv_b.reshape(1, -1), lmat, rmat, xmat, dch,
                    jnp.repeat(ssm_d, SSM_HEAD_DIM).reshape(1, SSM_INNER), ssm_norm_g.reshape(1, SSM_INNER),
                    bsz, seq)
    return _outproj_call(attn, ssm, proj, x2, mod, w_attn_o.astype(BF16), w_ssm_o.astype(BF16),
                         w_out.astype(BF16), seq)


ROUTE_BLOCK = 256
SLAB = SUBLANES
LOCAL_ROWS = ROUTE_BLOCK * TOP_K + ROUTE_BLOCK
LOCAL_SLABS = LOCAL_ROWS // SLAB
TILE_ROWS = 512
TILE_SLABS = TILE_ROWS // SLAB


U32 = jnp.uint32


def _pack_halves(a, b):
    return lax.bitcast_convert_type(b, U32) | (lax.bitcast_convert_type(a, U32) >> 16)


def _unpack_halves(p):
    return (lax.bitcast_convert_type(p << 16, F32), lax.bitcast_convert_type(p & jnp.uint32(0xFFFF0000), F32))


def _round_bf16(a):
    return a.astype(BF16).astype(F32)


def _route_kernel(x_ref, mod_ref, g_ref, rwh_ref, rwl_ref, rb_ref, xloc_ref, tokrow_ref, colw_ref, cnt_ref):
    tb = ROUTE_BLOCK
    h = _prenorm_modulate(x_ref[...], g_ref[...], mod_ref[0, 3:4, :], mod_ref[0, 4:5, :])
    hb = h.astype(BF16)
    h_lo = (h - hb.astype(F32)).astype(BF16)
    logits = (jnp.dot(hb, rwh_ref[...], preferred_element_type=F32)
              + jnp.dot(hb, rwl_ref[...], preferred_element_type=F32)
              + jnp.dot(h_lo, rwh_ref[...], preferred_element_type=F32)) + rb_ref[...]
    v = logits.T[0:N_EXPERTS, :]
    erow = lax.broadcasted_iota(jnp.int32, (N_EXPERTS, tb), 0)
    hots, tops = [], []
    for _ in range(TOP_K):
        mk = jnp.max(v, axis=0, keepdims=True)
        first = jnp.min(jnp.where(v == mk, erow, N_EXPERTS), axis=0, keepdims=True)
        hot = erow == first
        v = jnp.where(hot, -jnp.inf, v)
        hots.append(hot)
        tops.append(mk)
    ps = [jnp.exp(mk - tops[0]) for mk in tops]
    denom = ps[0] + ps[1] + ps[2] + ps[3]
    sel = jnp.zeros((N_EXPERTS, tb), F32)
    for hot in hots:
        sel = sel + jnp.where(hot, 1.0, 0.0)
    ti = lax.broadcasted_iota(jnp.int32, (tb, tb), 0)
    tj = lax.broadcasted_iota(jnp.int32, (tb, tb), 1)
    before = jnp.where(ti < tj, 1.0, 0.0).astype(BF16)
    rank = jnp.dot(sel.astype(BF16), before, preferred_element_type=F32)
    count = jnp.sum(sel, axis=1, keepdims=True)
    slabs = jnp.floor((count + (SLAB - 1)) * (1.0 / SLAB))
    ei = lax.broadcasted_iota(jnp.int32, (N_EXPERTS, N_EXPERTS), 0)
    ej = lax.broadcasted_iota(jnp.int32, (N_EXPERTS, N_EXPERTS), 1)
    earlier = jnp.where(ej < ei, 1.0, 0.0).astype(BF16)
    slabs_b = jnp.broadcast_to(slabs, (N_EXPERTS, tb))
    start = jnp.dot(earlier, slabs_b.astype(BF16), preferred_element_type=F32)
    dest = start * SLAB + rank
    dki = [jnp.sum(jnp.where(hot, dest, 0.0), axis=0, keepdims=True).astype(jnp.int32) for hot in hots]
    wks = [p / denom for p in ps]
    cnt_ref[0] = slabs_b[:, 0:LANES]
    tok1 = (lax.broadcasted_iota(jnp.int32, (SUBLANES, tb), 1) + 1).astype(BF16)
    for c in range(LOCAL_ROWS // tb):
        rs = slice(c * tb, (c + 1) * tb)
        ri = lax.broadcasted_iota(jnp.int32, (tb, tb), 0) + c * tb
        weighted = jnp.where(ri == dki[0], wks[0], jnp.where(ri == dki[1], wks[1], jnp.where(
            ri == dki[2], wks[2], jnp.where(ri == dki[3], wks[3], 0.0))))
        onehot = jnp.where(weighted > 0.0, 1.0, 0.0).astype(BF16)
        picked = jnp.dot(onehot, hb, preferred_element_type=F32)
        half = picked.shape[1] // 2
        xloc_ref[0, rs, :] = _pack_halves(picked[:, :half], picked[:, half:])
        colw_ref[0, rs, :] = jnp.broadcast_to(jnp.sum(weighted, axis=1, keepdims=True), (tb, LANES))
        tokrow_ref[0, :, rs] = lax.dot_general(tok1, onehot, (((1,), (1,)), ((), ())), preferred_element_type=F32)


def _route_call(x2, mod, gain, router_w, router_b, seq):
    t, d = x2.shape
    tb = ROUTE_BLOCK
    nblk = t // tb
    per_batch = seq // tb
    rw = _pad_lanes(router_w)
    rw_hi = rw.astype(BF16)
    return pl.pallas_call(
        _route_kernel,
        grid=(nblk,),
        in_specs=[
            pl.BlockSpec((tb, d), lambda i: (i, 0)),
            pl.BlockSpec((1, N_MOD, d), lambda i: (i // per_batch, 0, 0)),
            pl.BlockSpec((1, d), lambda i: (0, 0)),
            pl.BlockSpec((d, LANES), lambda i: (0, 0)),
            pl.BlockSpec((d, LANES), lambda i: (0, 0)),
            pl.BlockSpec((1, LANES), lambda i: (0, 0)),
        ],
        out_specs=[
            pl.BlockSpec((1, LOCAL_ROWS, d // 2), lambda i: (i, 0, 0)),
            pl.BlockSpec((1, SUBLANES, LOCAL_ROWS), lambda i: (i, 0, 0)),
            pl.BlockSpec((1, LOCAL_ROWS, LANES), lambda i: (i, 0, 0)),
            pl.BlockSpec((1, N_EXPERTS, LANES), lambda i: (i, 0, 0)),
        ],
        out_shape=[
            jax.ShapeDtypeStruct((nblk, LOCAL_ROWS, d // 2), U32),
            jax.ShapeDtypeStruct((nblk, SUBLANES, LOCAL_ROWS), F32),
            jax.ShapeDtypeStruct((nblk, LOCAL_ROWS, LANES), F32),
            jax.ShapeDtypeStruct((nblk, N_EXPERTS, LANES), F32),
        ],
        compiler_params=_params("arbitrary"),
        name="moe_route",
    )(x2, mod, gain, rw_hi, (rw - rw_hi.astype(F32)).astype(BF16), _pad_lanes(router_b.reshape(1, -1)))


def _slab_plan(slab_counts):
    nblk = slab_counts.shape[0]
    max_slabs = nblk * LOCAL_SLABS + N_EXPERTS * (TILE_SLABS - 1)
    max_tiles = -(-max_slabs // TILE_SLABS)
    c8 = slab_counts.astype(jnp.int32)
    local_start = jnp.cumsum(c8, axis=1) - c8
    per_expert = jnp.sum(c8, axis=0)
    tiles_e = (per_expert + TILE_SLABS - 1) // TILE_SLABS
    tile_start = jnp.cumsum(tiles_e) - tiles_e
    num_tiles = jnp.sum(tiles_e)
    expert_start = tile_start * TILE_SLABS
    block_off = jnp.cumsum(c8, axis=0) - c8
    seg_start = expert_start[None, :] + block_off

    sl = jnp.arange(LOCAL_SLABS, dtype=jnp.int32)[None, :, None]
    in_seg = (local_start[:, None, :] <= sl) & (sl < (local_start + c8)[:, None, :])
    slab_pos = jnp.sum(jnp.where(in_seg, (seg_start - local_start)[:, None, :] + sl, 0), axis=-1).reshape(-1)

    p = jnp.arange(max_tiles * TILE_SLABS, dtype=jnp.int32)[:, None]
    in_exp = (expert_start[None, :] <= p) & (p < (expert_start + per_expert)[None, :])
    off = p - jnp.sum(jnp.where(in_exp, expert_start[None, :], 0), axis=-1, keepdims=True)
    table = jnp.concatenate([block_off.T, c8.T, local_start.T], axis=1).astype(F32)
    picked = jnp.round(jnp.dot(in_exp.astype(F32), table, precision=HIGHEST)).astype(jnp.int32)
    boff, cnt, lst = picked[:, :nblk], picked[:, nblk:2 * nblk], picked[:, 2 * nblk:]
    in_blk = (boff <= off) & (off < boff + cnt)
    blk_base = jnp.arange(nblk, dtype=jnp.int32)[None, :] * LOCAL_SLABS
    slab_src = jnp.sum(jnp.where(in_blk, blk_base + lst - boff + off, 0), axis=-1)

    ti = jnp.arange(max_tiles, dtype=jnp.int32)
    tile_expert = jnp.clip(jnp.sum(tile_start[None, :] <= jnp.minimum(ti, num_tiles - 1)[:, None], axis=-1) - 1,
                           0, N_EXPERTS - 1)
    return tile_expert.astype(jnp.int32), slab_src.astype(jnp.int32), slab_pos.astype(jnp.int32), \
        num_tiles.reshape(1).astype(jnp.int32), max_tiles


def _expert_kernel(te_ref, src_ref, nt_ref, xloc_hbm, wg_ref, bg_ref, wu_ref, bu_ref, wd_ref, bd_ref, y_ref,
                   xbuf, sem, wg_s, wu_s, wd_s):
    i = pl.program_id(0)
    nt = nt_ref[0]

    def slab_copy(tile, slot, j):
        return pltpu.make_async_copy(xloc_hbm.at[src_ref[tile * TILE_SLABS + j]], xbuf.at[slot, j], sem.at[slot])

    def issue(tile, slot):
        for j in range(TILE_SLABS):
            slab_copy(tile, slot, j).start()

    @pl.when(i == 0)
    def _():
        issue(0, 0)

    @pl.when(i + 1 < nt)
    def _():
        issue(i + 1, (i + 1) % 2)

    @pl.when(i < nt)
    def _():
        slot = i % 2
        for j in range(TILE_SLABS):
            slab_copy(i, slot, j).wait()
        new_expert = jnp.logical_or(i == 0, te_ref[i] != te_ref[jnp.maximum(i - 1, 0)])

        @pl.when(new_expert)
        def _():
            rows = 128
            for src, dst in ((wg_ref, wg_s), (wu_ref, wu_s), (wd_ref, wd_s)):
                def cast(c, carry, src=src, dst=dst):
                    r = pl.ds(pl.multiple_of(c * rows, rows), rows)
                    dst[r, :] = src[0, 0, r, :].astype(BF16)
                    return carry
                lax.fori_loop(0, src.shape[2] // rows, cast, 0)

        x_lo, x_hi = _unpack_halves(xbuf[slot].reshape(TILE_ROWS, xbuf.shape[-1]))
        x = jnp.concatenate([x_lo.astype(BF16), x_hi.astype(BF16)], axis=1)
        gate = jnp.dot(x, wg_s[...], preferred_element_type=F32) + bg_ref[0, 0]
        up = jnp.dot(x, wu_s[...], preferred_element_type=F32) + bu_ref[0, 0]
        glu = jnp.minimum(gate, SWIGLU_LIMIT)
        lin = jnp.clip(up, -SWIGLU_LIMIT, SWIGLU_LIMIT)
        act = glu * (1.0 / (1.0 + jnp.exp(-SWIGLU_ALPHA * glu))) * (lin + 1.0)
        y = jnp.dot(act.astype(BF16), wd_s[...], preferred_element_type=F32) + bd_ref[0, 0]
        half = y.shape[1] // 2
        y_ref[...] = _pack_halves(_round_bf16(y[:, :half]), _round_bf16(y[:, half:]))

    @pl.when(i >= nt)
    def _():
        y_ref[...] = jnp.zeros_like(y_ref)


def _expert_call(tile_expert, slab_src, num_tiles, max_tiles, xloc, layer, w_gate, b_gate, w_up, b_up, w_down, b_down):
    nblk, _, packed = xloc.shape
    depth, _, d, ff = w_gate.shape
    xloc3 = xloc.reshape(nblk * LOCAL_SLABS, SLAB, packed)
    wspec = lambda k, n: pl.BlockSpec((1, 1, k, n), lambda i, te, ss, nt: (layer, te[i], 0, 0))
    return pl.pallas_call(
        _expert_kernel,
        grid_spec=pltpu.PrefetchScalarGridSpec(
            num_scalar_prefetch=3,
            grid=(max_tiles,),
            in_specs=[
                pl.BlockSpec(memory_space=pl.ANY),
                wspec(d, ff), wspec(1, ff), wspec(d, ff), wspec(1, ff), wspec(ff, d), wspec(1, d),
            ],
            out_specs=pl.BlockSpec((TILE_ROWS, d // 2), lambda i, te, ss, nt: (i, 0)),
            scratch_shapes=[
                pltpu.VMEM((2, TILE_SLABS, SLAB, packed), U32),
                pltpu.SemaphoreType.DMA((2,)),
                pltpu.VMEM((d, ff), BF16),
                pltpu.VMEM((d, ff), BF16),
                pltpu.VMEM((ff, d), BF16),
            ],
        ),
        out_shape=jax.ShapeDtypeStruct((max_tiles * TILE_ROWS, d // 2), U32),
        compiler_params=_params("arbitrary"),
        name="moe_experts",
    )(tile_expert, slab_src, num_tiles, xloc3, w_gate, b_gate.reshape(depth, N_EXPERTS, 1, ff), w_up,
      b_up.reshape(depth, N_EXPERTS, 1, ff), w_down, b_down.reshape(depth, N_EXPERTS, 1, d))


COMBINE_ISSUE_UNROLL = 8
COMBINE_LOOKAHEAD = 2


def _combine_kernel(pos_ref, y_hbm, tokrow_ref, colw_ref, x_ref, mod_ref, o_ref, ybuf, sem):
    b = pl.program_id(0)
    nb = pl.num_programs(0)
    tb = ROUTE_BLOCK
    d = x_ref.shape[-1]

    def slab_copy(blk, slot, j):
        return pltpu.make_async_copy(y_hbm.at[pos_ref[blk * LOCAL_SLABS + j]], ybuf.at[slot, j], sem.at[slot])

    def issue(blk, slot):
        def body(jj, carry):
            for u in range(COMBINE_ISSUE_UNROLL):
                slab_copy(blk, slot, jj * COMBINE_ISSUE_UNROLL + u).start()
            return carry
        lax.fori_loop(0, LOCAL_SLABS // COMBINE_ISSUE_UNROLL, body, 0)

    n_slots = COMBINE_LOOKAHEAD + 1

    @pl.when(b == 0)
    def _():
        for ahead in range(COMBINE_LOOKAHEAD):
            @pl.when(ahead < nb)
            def _():
                issue(ahead, ahead)

    @pl.when(b + COMBINE_LOOKAHEAD < nb)
    def _():
        issue(b + COMBINE_LOOKAHEAD, (b + COMBINE_LOOKAHEAD) % n_slots)

    slot = b % n_slots
    pltpu.make_async_copy(y_hbm.at[pl.ds(0, LOCAL_SLABS)], ybuf.at[slot], sem.at[slot]).wait()

    tok1 = (lax.broadcasted_iota(jnp.int32, (tb, tb), 0) + 1).astype(F32)
    acc = jnp.zeros((tb, d), F32)
    slabs_per_chunk = tb // SLAB
    for c in range(LOCAL_ROWS // tb):
        rs = slice(c * tb, (c + 1) * tb)
        y_lo, y_hi = _unpack_halves(ybuf[slot, c * slabs_per_chunk:(c + 1) * slabs_per_chunk].reshape(tb, d // 2))
        w = colw_ref[0, rs, :]
        yw = jnp.concatenate([part[:, t * LANES:(t + 1) * LANES] * w for part in (y_lo, y_hi)
                              for t in range(d // 2 // LANES)], axis=1).astype(BF16)
        unperm = jnp.where(tok1 == tokrow_ref[0, 0:1, rs], 1.0, 0.0).astype(BF16)
        acc = acc + jnp.dot(unperm, yw, preferred_element_type=F32)
    o_ref[...] = x_ref[...] + mod_ref[0, 5:6, :] * acc


def _combine_call(slab_pos, y, tokrow, colw, x2, mod, seq):
    t, d = x2.shape
    tb = ROUTE_BLOCK
    nblk = t // tb
    per_batch = seq // tb
    y3 = y.reshape(y.shape[0] // SLAB, SLAB, d // 2)
    return pl.pallas_call(
        _combine_kernel,
        grid_spec=pltpu.PrefetchScalarGridSpec(
            num_scalar_prefetch=1,
            grid=(nblk,),
            in_specs=[
                pl.BlockSpec(memory_space=pl.ANY),
                pl.BlockSpec((1, SUBLANES, LOCAL_ROWS), lambda i, sp: (i, 0, 0)),
                pl.BlockSpec((1, LOCAL_ROWS, LANES), lambda i, sp: (i, 0, 0)),
                pl.BlockSpec((tb, d), lambda i, sp: (i, 0)),
                pl.BlockSpec((1, N_MOD, d), lambda i, sp: (i // per_batch, 0, 0)),
            ],
            out_specs=pl.BlockSpec((tb, d), lambda i, sp: (i, 0)),
            scratch_shapes=[
                pltpu.VMEM((COMBINE_LOOKAHEAD + 1, LOCAL_SLABS, SLAB, d // 2), U32),
                pltpu.SemaphoreType.DMA((COMBINE_LOOKAHEAD + 1,)),
            ],
        ),
        out_shape=jax.ShapeDtypeStruct((t, d), F32),
        compiler_params=_params("arbitrary"),
        name="moe_combine",
    )(slab_pos, y3, tokrow, colw, x2, mod)


def _moe_layer(x2, mod, seq, layer, norm_g, router_w, router_b, w_gate, b_gate, w_up, b_up, w_down, b_down):
    d = x2.shape[1]
    xloc, tokrow, colw, cnt = _route_call(x2, mod, norm_g.reshape(1, d), router_w, router_b, seq)
    tile_expert, slab_src, slab_pos, num_tiles, max_tiles = _slab_plan(cnt[:, :, 0])
    y = _expert_call(tile_expert, slab_src, num_tiles, max_tiles, xloc, layer, w_gate, b_gate, w_up, b_up, w_down,
                     b_down)
    return _combine_call(slab_pos, y, tokrow, colw, x2, mod, seq)


def kernel(x, c, positions, ada_w, ada_b, norm1_g, norm2_g, w_in, q_norm_g, k_norm_g, attn_sink, conv_w, conv_b,
           a_log, dt_bias, ssm_d, ssm_norm_g, w_attn_o, w_ssm_o, w_out, router_w, router_b, exp_w_gate, exp_b_gate,
           exp_w_up, exp_b_up, exp_w_down, exp_b_down):
    bsz, seq, d = x.shape
    depth = ada_w.shape[0]
    assert d == D_MODEL and seq % 512 == 0 and (bsz * seq) % ROUTE_BLOCK == 0
    mod_all = _ada_call(c, ada_w, ada_b).reshape(depth, bsz, N_MOD, d)
    cos_t, sin_t = _rope_call(positions)
    x2 = x.reshape(bsz * seq, d)
    for l in range(depth):
        mod = mod_all[l]
        x2 = _mixer_layer(x2, mod, cos_t, sin_t, bsz, seq, norm1_g[l], w_in[l], q_norm_g[l], k_norm_g[l],
                          attn_sink[l], conv_w[l], conv_b[l], a_log[l], dt_bias[l], ssm_d[l], ssm_norm_g[l],
                          w_attn_o[l], w_ssm_o[l], w_out[l])
        x2 = _moe_layer(x2, mod, seq, l, norm2_g[l], router_w[l], router_b[l], exp_w_gate, exp_b_gate,
                        exp_w_up, exp_b_up, exp_w_down, exp_b_down)
    return x2.reshape(bsz, seq, d)
```
